```python
import jax
import jax.numpy as jnp
from jax import lax
import numpy as np

D_MODEL = 1024
BATCH = 8
SEQ = 4096
DEPTH = 2

MEM_LEN = 256
BRANCH_W = D_MODEL
NH_M = 4
DV_M = BRANCH_W // NH_M
DQK_M = DV_M // 2
MLSTM_CHUNK = 64
CONV_K = 4
SGU_GROUPS = 4
SGU_CHUNK = 128
D_SGU = BRANCH_W
NH_X = 4
DH_X = BRANCH_W // NH_X
N_BRANCH = 3
DFF = 256 * ((8 * D_MODEL // 3 + 255) // 256)
N_NORMS = 7
EPS = 1e-6
NEG_INIT = -1e30

W_IN_SIZES = (2 * NH_M * DQK_M, NH_M * DV_M, NH_M * DV_M, NH_M, NH_M, D_SGU, D_SGU, NH_X * DH_X, N_BRANCH * BRANCH_W)
W_IN_COLS = sum(W_IN_SIZES)

kernel_name = "hybrid_mlstm_sgu_memxattn_macaron"


def rmsnorm(x, g):
    xf = x.astype(jnp.float32)
    y = xf * lax.rsqrt(jnp.mean(xf * xf, axis=-1, keepdims=True) + EPS)
    return (y * g.astype(jnp.float32)).astype(x.dtype)


def layernorm(x, g, b=None):
    xf = x.astype(jnp.float32)
    mu = jnp.mean(xf, axis=-1, keepdims=True)
    xc = xf - mu
    y = xc * lax.rsqrt(jnp.mean(xc * xc, axis=-1, keepdims=True) + EPS) * g.astype(jnp.float32)
    if b is not None:
        y = y + b.astype(jnp.float32)
    return y.astype(x.dtype)


def swiglu(x, w_gu, w_d):
    a, b = jnp.split(x @ w_gu, 2, axis=-1)
    return (jax.nn.silu(a) * b) @ w_d


def split_cols(z):
    idx = [int(i) for i in np.cumsum(W_IN_SIZES)[:-1]]
    return jnp.split(z, idx, axis=-1)


def causal_dwconv(x, w, b):
    c = x.shape[-1]
    y = lax.conv_general_dilated(x, w[:, None, :].astype(x.dtype), window_strides=(1,),
                                 padding=[(CONV_K - 1, 0)],
                                 dimension_numbers=("NWC", "WIO", "NWC"),
                                 feature_group_count=c)
    return y + b


def mlstm_chunkwise(q, k, v, i_pre, f_pre):
    B_, S_, H, DK = q.shape
    DVv = v.shape[-1]
    NC = S_ // MLSTM_CHUNK
    f32 = jnp.float32

    def to_chunks(t):
        t = t.astype(f32).reshape((B_, NC, MLSTM_CHUNK, H) + t.shape[3:])
        return jnp.moveaxis(t, (1, 3), (0, 2))

    q_c = to_chunks(q)
    k_c = to_chunks(k) * (DK ** -0.5)
    v_c = to_chunks(v)
    i_c = to_chunks(i_pre)
    lf_c = jax.nn.log_sigmoid(to_chunks(f_pre))
    causal = jnp.tril(jnp.ones((MLSTM_CHUNK, MLSTM_CHUNK), dtype=bool))

    def step(carry, xs):
        C, n, m = carry
        qc, kc, vc, ic, fc = xs
        b = jnp.cumsum(fc, axis=-1)
        g = b[..., -1]
        log_d = jnp.where(causal, b[..., :, None] - b[..., None, :] + ic[..., None, :], -jnp.inf)
        m_inter = b + m[..., None]
        m_t = jnp.maximum(m_inter, jnp.max(log_d, axis=-1))
        d = jnp.exp(log_d - m_t[..., None])
        s = jnp.einsum("bhtd,bhsd->bhts", qc, kc) * d
        inter = jnp.exp(m_inter - m_t)
        num = jnp.einsum("bhts,bhsv->bhtv", s, vc) + inter[..., None] * jnp.einsum("bhtd,bhdv->bhtv", qc, C)
        den = jnp.sum(s, axis=-1) + inter * jnp.einsum("bhtd,bhd->bht", qc, n)
        h = num / jnp.maximum(jnp.abs(den), jnp.exp(-m_t))[..., None]
        a = g[..., None] - b + ic
        m_new = jnp.maximum(g + m, jnp.max(a, axis=-1))
        w = jnp.exp(a - m_new[..., None])
        decay = jnp.exp(g + m - m_new)
        C = decay[..., None, None] * C + jnp.einsum("bhs,bhsd,bhsv->bhdv", w, kc, vc)
        n = decay[..., None] * n + jnp.einsum("bhs,bhsd->bhd", w, kc)
        return (C, n, m_new), h

    init = (jnp.zeros((B_, H, DK, DVv), f32), jnp.zeros((B_, H, DK), f32),
            jnp.full((B_, H), NEG_INIT, f32))
    _, h = lax.scan(step, init, (q_c, k_c, v_c, i_c, lf_c))
    h = jnp.moveaxis(h, (0, 2), (1, 3)).reshape(B_, S_, H, DVv)
    return h.astype(v.dtype)


def spatial_gating(u, v, ln_g, ln_b, w_s, b_s):
    B_, S_, _ = u.shape
    nc = S_ // SGU_CHUNK
    dg = D_SGU // SGU_GROUPS
    shp = (B_, nc, SGU_CHUNK, SGU_GROUPS, dg)
    vn = layernorm(v.reshape(shp), ln_g.reshape(SGU_GROUPS, dg), ln_b.reshape(SGU_GROUPS, dg))
    w_causal = jnp.tril(w_s)
    mixed = jnp.einsum("gts,bcsgd->bctgd", w_causal, vn) + b_s.T[:, :, None]
    return (u.reshape(shp) * mixed).reshape(B_, S_, D_SGU)


def memory_cross_attention(xq, mem_n, w_mkv):
    B_, S_, _ = xq.shape
    q = xq.reshape(B_, S_, NH_X, DH_X)
    k, v = jnp.split(mem_n @ w_mkv, 2, axis=-1)
    k = k.reshape(B_, -1, NH_X, DH_X)
    v = v.reshape(B_, -1, NH_X, DH_X)
    s = jnp.einsum("bshd,bmhd->bhsm", q, k).astype(jnp.float32) * (DH_X ** -0.5)
    p = jax.nn.softmax(s, axis=-1).astype(v.dtype)
    return jnp.einsum("bhsm,bmhd->bshd", p, v).reshape(B_, S_, NH_X * DH_X)


def setup_inputs(seed: int = 0) -> dict:
    key = jax.random.key(seed)
    ks = jax.random.split(key, 18)
    nrm = jax.random.normal
    f32 = jnp.float32
    return {
        "x": nrm(ks[0], (BATCH, SEQ, D_MODEL), f32),
        "mem": nrm(ks[1], (BATCH, MEM_LEN, D_MODEL), f32),
        "norm_g": 1.0 + 0.05 * nrm(ks[2], (DEPTH, N_NORMS, D_MODEL), f32),
        "w_ffn_gu": nrm(ks[3], (DEPTH, 2, D_MODEL, 2 * DFF), f32) * D_MODEL ** -0.5,
        "w_ffn_down": nrm(ks[4], (DEPTH, 2, DFF, D_MODEL), f32) * DFF ** -0.5,
        "w_in": nrm(ks[5], (DEPTH, D_MODEL, W_IN_COLS), f32) * D_MODEL ** -0.5,
        "conv_w": nrm(ks[6], (DEPTH, CONV_K, 2 * NH_M * DQK_M), f32) * CONV_K ** -0.5,
        "conv_b": 0.02 * nrm(ks[7], (DEPTH, 2 * NH_M * DQK_M), f32),
        "b_i": 0.1 * nrm(ks[8], (DEPTH, NH_M), f32),
        "b_f": 3.0 + 0.5 * nrm(ks[9], (DEPTH, NH_M), f32),
        "mlstm_norm_g": 1.0 + 0.05 * nrm(ks[10], (DEPTH, NH_M * DV_M), f32),
        "sgu_ln_g": 1.0 + 0.05 * nrm(ks[11], (DEPTH, D_SGU), f32),
        "sgu_ln_b": 0.02 * nrm(ks[12], (DEPTH, D_SGU), f32),
        "sgu_w_s": nrm(ks[13], (DEPTH, SGU_GROUPS, SGU_CHUNK, SGU_CHUNK), f32) * SGU_CHUNK ** -0.5,
        "sgu_b_s": 1.0 + 0.1 * nrm(ks[14], (DEPTH, SGU_GROUPS, SGU_CHUNK), f32),
        "w_mkv": nrm(ks[15], (DEPTH, D_MODEL, 2 * NH_X * DH_X), f32) * D_MODEL ** -0.5,
        "w_branch": nrm(ks[16], (DEPTH, N_BRANCH, BRANCH_W, D_MODEL), f32) * BRANCH_W ** -0.5,
        "w_out": nrm(ks[17], (DEPTH, D_MODEL, D_MODEL), f32) * D_MODEL ** -0.5,
    }


def reference(x, mem, norm_g, w_ffn_gu, w_ffn_down, w_in, conv_w, conv_b, b_i, b_f,
              mlstm_norm_g, sgu_ln_g, sgu_ln_b, sgu_w_s, sgu_b_s, w_mkv, w_branch, w_out):
    h = x
    bs = x.shape[:2]
    for l in range(DEPTH):
        g = norm_g[l]
        f = swiglu(rmsnorm(h, g[0]), w_ffn_gu[l, 0], w_ffn_down[l, 0])
        h = h + 0.5 * rmsnorm(f, g[1])

        xn = rmsnorm(h, g[2])
        z = xn @ w_in[l]
        zqk, zv, zo, zi, zf, zu, zsv, zxq, zg = split_cols(z)

        qk = jax.nn.silu(causal_dwconv(zqk, conv_w[l], conv_b[l]))
        q, k = jnp.split(qk, 2, axis=-1)
        y_m = mlstm_chunkwise(q.reshape(bs + (NH_M, DQK_M)), k.reshape(bs + (NH_M, DQK_M)),
                              zv.reshape(bs + (NH_M, DV_M)), zi + b_i[l], zf + b_f[l])
        y_m = layernorm(y_m, mlstm_norm_g[l].reshape(NH_M, DV_M)).reshape(bs + (NH_M * DV_M,))
        y_m = y_m * jax.nn.sigmoid(zo)

        y_s = spatial_gating(jax.nn.gelu(zu), jax.nn.gelu(zsv), sgu_ln_g[l], sgu_ln_b[l],
                             sgu_w_s[l], sgu_b_s[l])

        y_x = memory_cross_attention(zxq, rmsnorm(mem, g[6]), w_mkv[l])

        gates = jax.nn.sigmoid(zg).reshape(bs + (N_BRANCH, D_MODEL))
        merged = (gates[..., 0, :] * (y_m @ w_branch[l, 0])
                  + gates[..., 1, :] * (y_s @ w_branch[l, 1])
                  + gates[..., 2, :] * (y_x @ w_branch[l, 2]))
        y = merged @ w_out[l]
        h = h + rmsnorm(y, g[3])

        f = swiglu(rmsnorm(h, g[4]), w_ffn_gu[l, 1], w_ffn_down[l, 1])
        h = h + 0.5 * rmsnorm(f, g[5])
    return h
```

```python
import functools

import jax
import jax.numpy as jnp
from jax import lax
from jax.experimental import pallas as pl
from jax.experimental.pallas import tpu as pltpu

F32 = jnp.float32
BF16 = jnp.bfloat16

EPS = 1e-6
NEG_INIT = -1e30
NH_M = 4
NH_X = 4
SGU_GROUPS = 4
SGU_CHUNK = 128
CONV_K = 4
N_BRANCH = 3
N_NORMS = 7
N_Z_BLOCKS = 9
GATE_LANES = 128

VMEM_LIMIT_V7X = 56 * 1024 * 1024

FFN_TM = 512
FFN_CW = 256
PROJ_TM = 512
MIX_TM = 512
MLSTM_TM = 512
MLSTM_L = 256


def _params(semantics):
    return pltpu.CompilerParams(dimension_semantics=semantics, vmem_limit_bytes=VMEM_LIMIT_V7X)


def _resident(block_shape, index_map):
    return pl.BlockSpec(block_shape, index_map, pipeline_mode=pl.Buffered(1))


def _rms(x, g):
    return x * lax.rsqrt(jnp.mean(x * x, axis=-1, keepdims=True) + EPS) * g


def _dot(a, b):
    return jnp.dot(a, b, preferred_element_type=F32)


def _ffn_body(x_ref, gpre_ref, gpost_ref, wgu_ref, wd_ref, o_ref, hid_ref, *, dff, cw):
    x = x_ref[...]
    xn = _rms(x, gpre_ref[...]).astype(BF16)
    for c in range(dff // cw):
        ab = _dot(xn, wgu_ref[:, 2 * c * cw:2 * (c + 1) * cw])
        a = ab[:, :cw]
        hid_ref[:, c * cw:(c + 1) * cw] = (a * jax.nn.sigmoid(a) * ab[:, cw:]).astype(BF16)
    f = _dot(hid_ref[...], wd_ref[...])
    o_ref[...] = x + 0.5 * _rms(f, gpost_ref[...])


def _ffn(h, norms, l, k_pre, k_post, wgu, wd, ffn_idx):
    n, d = h.shape
    dff = wd.shape[2]
    tm = min(FFN_TM, n)
    return pl.pallas_call(
        functools.partial(_ffn_body, dff=dff, cw=FFN_CW),
        grid=(n // tm,),
        in_specs=[
            pl.BlockSpec((tm, d), lambda i: (i, 0)),
            _resident((None, 1, d), lambda i: (l * N_NORMS + k_pre, 0, 0)),
            _resident((None, 1, d), lambda i: (l * N_NORMS + k_post, 0, 0)),
            _resident((None, None, d, 2 * dff), lambda i: (l, ffn_idx, 0, 0)),
            _resident((None, None, dff, d), lambda i: (l, ffn_idx, 0, 0)),
        ],
        out_specs=pl.BlockSpec((tm, d), lambda i: (i, 0)),
        out_shape=jax.ShapeDtypeStruct((n, d), F32),
        scratch_shapes=[pltpu.VMEM((tm, dff), BF16)],
        compiler_params=_params(("parallel",)),
        name="ffn",
    )(h, norms, norms, wgu, wd)


def _inproj_body(x_ref, g_ref, w_ref, wif_ref, z_ref, gate_ref, *, d):
    xn = _rms(x_ref[...], g_ref[...]).astype(BF16)
    for c in range(N_Z_BLOCKS):
        z_ref[:, c * d:(c + 1) * d] = _dot(xn, w_ref[:, c * d:(c + 1) * d]).astype(BF16)
    gate_ref[...] = _dot(xn, wif_ref[...])


def _inproj(h, norms, l, w_main, w_if):
    n, d = h.shape
    tm = min(PROJ_TM, n)
    return pl.pallas_call(
        functools.partial(_inproj_body, d=d),
        grid=(n // tm,),
        in_specs=[
            pl.BlockSpec((tm, d), lambda i: (i, 0)),
            _resident((None, 1, d), lambda i: (l * N_NORMS + 2, 0, 0)),
            _resident((None, d, N_Z_BLOCKS * d), lambda i: (l, 0, 0)),
            _resident((None, d, GATE_LANES), lambda i: (l, 0, 0)),
        ],
        out_specs=[
            pl.BlockSpec((tm, N_Z_BLOCKS * d), lambda i: (i, 0)),
            pl.BlockSpec((tm, GATE_LANES), lambda i: (i, 0)),
        ],
        out_shape=[
            jax.ShapeDtypeStruct((n, N_Z_BLOCKS * d), BF16),
            jax.ShapeDtypeStruct((n, GATE_LANES), F32),
        ],
        compiler_params=_params(("parallel",)),
        name="inproj",
    )(h, norms, w_main, w_if)


def _memkv_body(mem_ref, g_ref, wkt_ref, wv_ref, kt_ref, v_ref, *, scale):
    mn = _rms(mem_ref[...], g_ref[...]).astype(BF16)
    kt = lax.dot_general(wkt_ref[...], mn, (((1,), (1,)), ((), ())), preferred_element_type=F32)
    kt_ref[...] = (kt * scale).astype(BF16)
    v_ref[...] = _dot(mn, wv_ref[...]).astype(BF16)


def _memkv(mem, norms, l, wkt, wv):
    b, m, d = mem.shape
    scale = float((d // NH_X) ** -0.5)
    return pl.pallas_call(
        functools.partial(_memkv_body, scale=scale),
        grid=(b,),
        in_specs=[
            pl.BlockSpec((None, m, d), lambda i: (i, 0, 0)),
            _resident((None, 1, d), lambda i: (l * N_NORMS + 6, 0, 0)),
            _resident((None, d, d), lambda i: (l, 0, 0)),
            _resident((None, d, d), lambda i: (l, 0, 0)),
        ],
        out_specs=[
            pl.BlockSpec((None, d, m), lambda i: (i, 0, 0)),
            pl.BlockSpec((None, m, d), lambda i: (i, 0, 0)),
        ],
        out_shape=[
            jax.ShapeDtypeStruct((b, d, m), BF16),
            jax.ShapeDtypeStruct((b, m, d), BF16),
        ],
        compiler_params=_params(("parallel",)),
        name="memkv",
    )(mem, norms, wkt, wv)


def _log_sigmoid(x):
    return -(jnp.maximum(-x, 0.0) + jnp.log1p(jnp.exp(-jnp.abs(x))))


def _mlstm_body(zqk_ref, prev_ref, zv_ref, zo_ref, gate_ref, cw_ref, cb_ref, bif_ref, ng_ref,
                y_ref, xs_ref, q_ref, k_ref, c_ref, n_ref, m_ref, *, tm, chunk, dqk, dv):
    i = pl.program_id(1)
    nqk = NH_M * dqk

    @pl.when(i == 0)
    def _():
        c_ref[...] = jnp.zeros_like(c_ref)
        n_ref[...] = jnp.zeros_like(n_ref)
        m_ref[...] = jnp.full_like(m_ref, NEG_INIT)

    hist = jnp.where(i == 0, 0.0, prev_ref[...].astype(F32))
    xs_ref[0:8, :] = hist
    xs_ref[8:, :] = zqk_ref[...].astype(F32)
    conv = cb_ref[...]
    for j in range(CONV_K):
        conv = conv + cw_ref[j:j + 1, :] * xs_ref[pl.ds(8 - (CONV_K - 1) + j, tm), :]
    qk = conv * jax.nn.sigmoid(conv)
    q_ref[...] = qk[:, :nqk]
    k_ref[...] = qk[:, nqk:] * float(dqk ** -0.5)

    gp = gate_ref[...] + bif_ref[...]
    lane = lax.broadcasted_iota(jnp.int32, gp.shape, 1)
    gates = jnp.where(lane < NH_M, gp, _log_sigmoid(gp))
    gates_t = gates.T

    r_i = lax.broadcasted_iota(jnp.int32, (chunk, chunk), 0)
    c_i = lax.broadcasted_iota(jnp.int32, (chunk, chunk), 1)
    causal = r_i >= c_i
    tril = causal.astype(F32)
    triu = (r_i <= c_i).astype(F32)

    for c in range(tm // chunk):
        rows = slice(c * chunk, (c + 1) * chunk)
        g_c = gates[rows, :]
        g_r = gates_t[0:2 * NH_M, rows]
        cum_c = jnp.dot(tril, g_c, precision=lax.Precision.HIGHEST, preferred_element_type=F32)
        cum_r = jnp.dot(g_r, triu, precision=lax.Precision.HIGHEST, preferred_element_type=F32)
        for h in range(NH_M):
            qc = q_ref[rows, h * dqk:(h + 1) * dqk]
            kc = k_ref[rows, h * dqk:(h + 1) * dqk]
            vc = zv_ref[rows, h * dv:(h + 1) * dv]
            qb = qc.astype(BF16)
            i_col = g_c[:, h:h + 1]
            b_col = cum_c[:, NH_M + h:NH_M + h + 1]
            i_row = g_r[h:h + 1, :]
            b_row = cum_r[NH_M + h:NH_M + h + 1, :]
            m_prev = m_ref[h, :, 0:1]
            g_tot = b_col[chunk - 1:chunk, :]

            log_d = jnp.where(causal, b_col + (i_row - b_row), -jnp.inf)
            m_inter = b_col + m_prev
            m_t = jnp.maximum(m_inter, jnp.max(log_d, axis=-1, keepdims=True))
            d_mat = jnp.exp(log_d - m_t)
            s = lax.dot_general(qb, kc.astype(BF16), (((1,), (1,)), ((), ())),
                                preferred_element_type=F32) * d_mat
            inter = jnp.exp(m_inter - m_t)
            num = _dot(s.astype(BF16), vc) + inter * _dot(qb, c_ref[h].astype(BF16))
            den = (jnp.sum(s, axis=-1, keepdims=True)
                   + inter * jnp.sum(qc * n_ref[h], axis=-1, keepdims=True))
            hh = num / jnp.maximum(jnp.abs(den), jnp.exp(-m_t))

            a_col = g_tot - b_col + i_col
            m_new = jnp.maximum(g_tot + m_prev, jnp.max(a_col, axis=0, keepdims=True))
            wk = jnp.exp(a_col - m_new) * kc
            decay = jnp.exp(g_tot + m_prev - m_new)
            c_ref[h] = decay * c_ref[h] + lax.dot_general(
                wk.astype(BF16), vc, (((0,), (0,)), ((), ())), preferred_element_type=F32)
            n_ref[h] = decay * n_ref[h] + jnp.sum(wk, axis=0, keepdims=True)
            m_ref[h] = jnp.broadcast_to(m_new, m_ref.shape[1:])

            mu = jnp.mean(hh, axis=-1, keepdims=True)
            hc = hh - mu
            hn = hc * lax.rsqrt(jnp.mean(hc * hc, axis=-1, keepdims=True) + EPS)
            hn = hn * ng_ref[:, h * dv:(h + 1) * dv]
            og = jax.nn.sigmoid(zo_ref[rows, h * dv:(h + 1) * dv].astype(F32))
            y_ref[rows, h * dv:(h + 1) * dv] = (hn * og).astype(BF16)


def _mlstm(z, gate_pre, b, s, conv_w, conv_b, b_if, norm_g, l):
    n = z.shape[0]
    d = z.shape[1] // N_Z_BLOCKS
    dv = d // NH_M
    dqk = dv // 2
    tm = min(MLSTM_TM, s)
    chunk = min(MLSTM_L, tm)
    nt = s // tm
    return pl.pallas_call(
        functools.partial(_mlstm_body, tm=tm, chunk=chunk, dqk=dqk, dv=dv),
        grid=(b, nt),
        in_specs=[
            pl.BlockSpec((tm, d), lambda bi, i: (bi * nt + i, 0)),
            pl.BlockSpec((8, d), lambda bi, i: (jnp.maximum((bi * nt + i) * (tm // 8) - 1, 0), 0)),
            pl.BlockSpec((tm, d), lambda bi, i: (bi * nt + i, 1)),
            pl.BlockSpec((tm, d), lambda bi, i: (bi * nt + i, 2)),
            pl.BlockSpec((tm, GATE_LANES), lambda bi, i: (bi * nt + i, 0)),
            _resident((None, CONV_K, d), lambda bi, i: (l, 0, 0)),
            _resident((None, 1, d), lambda bi, i: (l, 0, 0)),
            _resident((None, 1, GATE_LANES), lambda bi, i: (l, 0, 0)),
            _resident((None, 1, d), lambda bi, i: (l, 0, 0)),
        ],
        out_specs=pl.BlockSpec((tm, d), lambda bi, i: (bi * nt + i, 0)),
        out_shape=jax.ShapeDtypeStruct((n, d), BF16),
        scratch_shapes=[
            pltpu.VMEM((tm + 8, d), F32),
            pltpu.VMEM((tm, NH_M * dqk), F32),
            pltpu.VMEM((tm, NH_M * dqk), F32),
            pltpu.VMEM((NH_M, dqk, dv), F32),
            pltpu.VMEM((NH_M, 1, dqk), F32),
            pltpu.VMEM((NH_M, 1, GATE_LANES), F32),
        ],
        compiler_params=_params(("arbitrary", "arbitrary")),
        name="mlstm",
    )(z, z, z, z, gate_pre, conv_w, conv_b, b_if, norm_g)


def _sgu_body(zu_ref, zv_ref, lng_ref, lnb_ref, ws_ref, bs_ref, y_ref, *, tm, dg):
    r_i = lax.broadcasted_iota(jnp.int32, (SGU_CHUNK, SGU_CHUNK), 0)
    c_i = lax.broadcasted_iota(jnp.int32, (SGU_CHUNK, SGU_CHUNK), 1)
    causal = r_i >= c_i
    for g in range(SGU_GROUPS):
        cols = slice(g * dg, (g + 1) * dg)
        v = jax.nn.gelu(zv_ref[:, cols].astype(F32))
        mu = jnp.mean(v, axis=-1, keepdims=True)
        vc = v - mu
        vn = vc * lax.rsqrt(jnp.mean(vc * vc, axis=-1, keepdims=True) + EPS) * lng_ref[:, cols]
        vn = (vn + lnb_ref[:, cols]).astype(BF16)
        w = jnp.where(causal, ws_ref[g], 0.0).astype(BF16)
        for c in range(tm // SGU_CHUNK):
            rows = slice(c * SGU_CHUNK, (c + 1) * SGU_CHUNK)
            mixed = _dot(w, vn[rows, :]) + bs_ref[g]
            u = jax.nn.gelu(zu_ref[rows, cols].astype(F32))
            y_ref[rows, cols] = (u * mixed).astype(BF16)


def _sgu(z, ln_g, ln_b, w_s, b_s, l):
    n = z.shape[0]
    d = z.shape[1] // N_Z_BLOCKS
    tm = min(MIX_TM, n)
    return pl.pallas_call(
        functools.partial(_sgu_body, tm=tm, dg=d // SGU_GROUPS),
        grid=(n // tm,),
        in_specs=[
            pl.BlockSpec((tm, d), lambda i: (i, 3)),
            pl.BlockSpec((tm, d), lambda i: (i, 4)),
            _resident((None, 1, d), lambda i: (l, 0, 0)),
            _resident((None, 1, d), lambda i: (l, 0, 0)),
            _resident((None, SGU_GROUPS, SGU_CHUNK, SGU_CHUNK), lambda i: (l, 0, 0, 0)),
            _resident((None, SGU_GROUPS, SGU_CHUNK, 1), lambda i: (l, 0, 0, 0)),
        ],
        out_specs=pl.BlockSpec((tm, d), lambda i: (i, 0)),
        out_shape=jax.ShapeDtypeStruct((n, d), BF16),
        compiler_params=_params(("parallel",)),
        name="sgu",
    )(z, z, ln_g, ln_b, w_s, b_s)


def _xattn_body(zq_ref, kt_ref, v_ref, y_ref, *, dh):
    for h in range(NH_X):
        cols = slice(h * dh, (h + 1) * dh)
        s = _dot(zq_ref[:, cols], kt_ref[cols, :])
        p = jnp.exp(s - jnp.max(s, axis=-1, keepdims=True))
        p = p / jnp.sum(p, axis=-1, keepdims=True)
        y_ref[:, cols] = _dot(p.astype(BF16), v_ref[:, cols]).astype(BF16)


def _xattn(z, kt, v, b, s):
    n = z.shape[0]
    d = z.shape[1] // N_Z_BLOCKS
    m = v.shape[1]
    tm = min(MIX_TM, s)
    nt = s // tm
    return pl.pallas_call(
        functools.partial(_xattn_body, dh=d // NH_X),
        grid=(b, nt),
        in_specs=[
            pl.BlockSpec((tm, d), lambda bi, i: (bi * nt + i, 5)),
            pl.BlockSpec((None, d, m), lambda bi, i: (bi, 0, 0)),
            pl.BlockSpec((None, m, d), lambda bi, i: (bi, 0, 0)),
        ],
        out_specs=pl.BlockSpec((tm, d), lambda bi, i: (bi * nt + i, 0)),
        out_shape=jax.ShapeDtypeStruct((n, d), BF16),
        compiler_params=_params(("parallel", "parallel")),
        name="xattn",
    )(z, kt, v)


def _merge_body(h_ref, zg_ref, ym_ref, ys_ref, yx_ref, wb_ref, wo_ref, g_ref, o_ref, *, d):
    merged = None
    for j, y_ref in enumerate((ym_ref, ys_ref, yx_ref)):
        gate = jax.nn.sigmoid(zg_ref[:, j * d:(j + 1) * d].astype(F32))
        term = gate * _dot(y_ref[...], wb_ref[j])
        merged = term if merged is None else merged + term
    y = _dot(merged.astype(BF16), wo_ref[...])
    o_ref[...] = h_ref[...] + _rms(y, g_ref[...])


def _merge(h, z, y_m, y_s, y_x, w_branch, w_out, norms, l):
    n, d = h.shape
    tm = min(MIX_TM, n)
    tok = pl.BlockSpec((tm, d), lambda i: (i, 0))
    return pl.pallas_call(
        functools.partial(_merge_body, d=d),
        grid=(n // tm,),
        in_specs=[
            tok,
            pl.BlockSpec((tm, N_BRANCH * d), lambda i: (i, 2)),
            tok, tok, tok,
            _resident((None, N_BRANCH, d, d), lambda i: (l, 0, 0, 0)),
            _resident((None, d, d), lambda i: (l, 0, 0)),
            _resident((None, 1, d), lambda i: (l * N_NORMS + 3, 0, 0)),
        ],
        out_specs=tok,
        out_shape=jax.ShapeDtypeStruct((n, d), F32),
        compiler_params=_params(("parallel",)),
        name="merge",
    )(h, z, y_m, y_s, y_x, w_branch, w_out, norms)


def kernel(x, mem, norm_g, w_ffn_gu, w_ffn_down, w_in, conv_w, conv_b, b_i, b_f, mlstm_norm_g,
           sgu_ln_g, sgu_ln_b, sgu_w_s, sgu_b_s, w_mkv, w_branch, w_out):
    b, s, d = x.shape
    depth = norm_g.shape[0]
    dff = w_ffn_down.shape[2]
    n = b * s
    assert s % MLSTM_TM == 0 or s < MLSTM_TM
    assert dff % FFN_CW == 0

    norms = norm_g.reshape(depth * N_NORMS, 1, d)
    nch = dff // FFN_CW
    wgu = (w_ffn_gu.reshape(depth, 2, d, 2, nch, FFN_CW).transpose(0, 1, 2, 4, 3, 5)
           .reshape(depth, 2, d, 2 * dff).astype(BF16))
    wd = w_ffn_down.astype(BF16)
    n_gate = 2 * NH_M
    c_if = 3 * d
    w_main = jnp.concatenate([w_in[:, :, :c_if], w_in[:, :, c_if + n_gate:]], axis=-1).astype(BF16)
    w_if = jnp.pad(w_in[:, :, c_if:c_if + n_gate], ((0, 0), (0, 0), (0, GATE_LANES - n_gate))).astype(BF16)
    b_if = jnp.pad(jnp.concatenate([b_i, b_f], axis=-1), ((0, 0), (0, GATE_LANES - n_gate)))
    b_if = b_if.reshape(depth, 1, GATE_LANES)
    wkt = jnp.swapaxes(w_mkv[:, :, :d], 1, 2).astype(BF16)
    wv = w_mkv[:, :, d:].astype(BF16)
    wb = w_branch.astype(BF16)
    wo = w_out.astype(BF16)
    conv_b3 = conv_b.reshape(depth, 1, d)
    ng3 = mlstm_norm_g.reshape(depth, 1, d)
    lng3 = sgu_ln_g.reshape(depth, 1, d)
    lnb3 = sgu_ln_b.reshape(depth, 1, d)
    bs4 = sgu_b_s.reshape(depth, SGU_GROUPS, SGU_CHUNK, 1)

    h = x.reshape(n, d)
    for l in range(depth):
        h = _ffn(h, norms, l, 0, 1, wgu, wd, 0)
        z, gate_pre = _inproj(h, norms, l, w_main, w_if)
        kt, v = _memkv(mem, norms, l, wkt, wv)
        y_m = _mlstm(z, gate_pre, b, s, conv_w, conv_b3, b_if, ng3, l)
        y_s = _sgu(z, lng3, lnb3, sgu_w_s, bs4, l)
        y_x = _xattn(z, kt, v, b, s)
        h = _merge(h, z, y_m, y_s, y_x, wb, wo, norms, l)
        h = _ffn(h, norms, l, 4, 5, wgu, wd, 1)
    return h.reshape(b, s, d)
```

```python
import functools

import jax
import jax.numpy as jnp
from jax import lax
from jax.experimental import pallas as pl
from jax.experimental.pallas import tpu as pltpu

F32 = jnp.float32
BF16 = jnp.bfloat16

EPS = 1e-6
NEG_INIT = -1e30
NH_M = 4
NH_X = 4
SGU_GROUPS = 4
SGU_CHUNK = 128
CONV_K = 4
N_BRANCH = 3
N_NORMS = 7
N_Z_BLOCKS = 9
N_Z_MLSTM = 3
N_Z_REST = N_Z_BLOCKS - N_Z_MLSTM
GATE_LANES = 128
HIST = 8

VMEM_LIMIT_V7X = 56 * 1024 * 1024

FFN_TM = 512
FFN_CW = 256
MIX_TM = 512
MLSTM_L = 256


def _params(semantics):
    return pltpu.CompilerParams(dimension_semantics=semantics, vmem_limit_bytes=VMEM_LIMIT_V7X)


def _resident(block_shape, index_map):
    return pl.BlockSpec(block_shape, index_map, pipeline_mode=pl.Buffered(1))


def _rms(x, g):
    return x * lax.rsqrt(jnp.mean(x * x, axis=-1, keepdims=True) + EPS) * g


def _dot(a, b):
    return jnp.dot(a, b, preferred_element_type=F32)


def _ffn_body(x_ref, gpre_ref, gpost_ref, wgu_ref, wd_ref, o_ref, hid_ref, *, dff, cw):
    x = x_ref[...]
    xn = _rms(x, gpre_ref[...]).astype(BF16)
    for c in range(dff // cw):
        a = _dot(xn, wgu_ref[:, c * cw:(c + 1) * cw])
        b = _dot(xn, wgu_ref[:, dff + c * cw:dff + (c + 1) * cw])
        hid_ref[:, c * cw:(c + 1) * cw] = (a * jax.nn.sigmoid(a) * b).astype(BF16)
    f = _dot(hid_ref[...], wd_ref[...])
    o_ref[...] = x + 0.5 * _rms(f, gpost_ref[...])


def _ffn(h, norms, l, k_pre, k_post, wgu, wd, ffn_idx):
    n, d = h.shape
    dff = wd.shape[2]
    tm = min(FFN_TM, n)
    return pl.pallas_call(
        functools.partial(_ffn_body, dff=dff, cw=FFN_CW),
        grid=(n // tm,),
        in_specs=[
            pl.BlockSpec((tm, d), lambda i: (i, 0)),
            _resident((None, 1, d), lambda i: (l * N_NORMS + k_pre, 0, 0)),
            _resident((None, 1, d), lambda i: (l * N_NORMS + k_post, 0, 0)),
            _resident((None, None, d, 2 * dff), lambda i: (l, ffn_idx, 0, 0)),
            _resident((None, None, dff, d), lambda i: (l, ffn_idx, 0, 0)),
        ],
        out_specs=pl.BlockSpec((tm, d), lambda i: (i, 0)),
        out_shape=jax.ShapeDtypeStruct((n, d), F32),
        scratch_shapes=[pltpu.VMEM((tm, dff), BF16)],
        compiler_params=_params(("parallel",)),
        name="ffn",
    )(h, norms, norms, wgu, wd)


def _memkv_body(mem_ref, g_ref, wkv_ref, kt_ref, v_ref, *, d, scale):
    mn = _rms(mem_ref[...], g_ref[...]).astype(BF16)
    k = _dot(mn, wkv_ref[:, :d]) * scale
    kt_ref[...] = k.T.astype(BF16)
    v_ref[...] = _dot(mn, wkv_ref[:, d:]).astype(BF16)


def _memkv(mem, norms, l, wkv):
    b, m, d = mem.shape
    scale = float((d // NH_X) ** -0.5)
    return pl.pallas_call(
        functools.partial(_memkv_body, d=d, scale=scale),
        grid=(b,),
        in_specs=[
            pl.BlockSpec((None, m, d), lambda i: (i, 0, 0)),
            _resident((None, 1, d), lambda i: (l * N_NORMS + 6, 0, 0)),
            _resident((None, d, 2 * d), lambda i: (l, 0, 0)),
        ],
        out_specs=[
            pl.BlockSpec((None, d, m), lambda i: (i, 0, 0)),
            pl.BlockSpec((None, m, d), lambda i: (i, 0, 0)),
        ],
        out_shape=[
            jax.ShapeDtypeStruct((b, d, m), BF16),
            jax.ShapeDtypeStruct((b, m, d), BF16),
        ],
        compiler_params=_params(("parallel",)),
        name="memkv",
    )(mem, norms, wkv)


def _log_sigmoid(x):
    return -(jnp.maximum(-x, 0.0) + jnp.log1p(jnp.exp(-jnp.abs(x))))


def _mixa_body(h_ref, g_ref, w_ref, wif_ref, cw_ref, cb_ref, bif_ref, ng_ref,
               y_ref, z_ref,
               xn_ref, xs_ref, q_ref, k_ref, v_ref, o_ref, c_ref, n_ref, m_ref,
               *, tm, chunk, d, dqk, dv):
    i = pl.program_id(1)
    nqk = NH_M * dqk
    n_units = (tm // chunk) * NH_M
    zw = N_Z_REST * d // n_units

    @pl.when(i == 0)
    def _():
        c_ref[...] = jnp.zeros_like(c_ref)
        n_ref[...] = jnp.zeros_like(n_ref)
        m_ref[...] = jnp.full_like(m_ref, NEG_INIT)
        xs_ref[0:HIST, :] = jnp.zeros((HIST, d), F32)

    @pl.when(i > 0)
    def _():
        xs_ref[0:HIST, :] = xs_ref[tm:tm + HIST, :]

    xn_ref[...] = _rms(h_ref[...], g_ref[...]).astype(BF16)
    xs_ref[HIST:, :] = _dot(xn_ref[...], w_ref[:, 0:d])
    v_ref[...] = _dot(xn_ref[...], w_ref[:, d:2 * d]).astype(BF16)
    o_ref[...] = _dot(xn_ref[...], w_ref[:, 2 * d:3 * d]).astype(BF16)
    gp = _dot(xn_ref[...], wif_ref[...]) + bif_ref[...]

    conv = cb_ref[...]
    for j in range(CONV_K):
        conv = conv + cw_ref[j:j + 1, :] * xs_ref[pl.ds(HIST - (CONV_K - 1) + j, tm), :]
    qk = conv * jax.nn.sigmoid(conv)
    q_ref[...] = qk[:, :nqk]
    k_ref[...] = qk[:, nqk:] * float(dqk ** -0.5)

    lane = lax.broadcasted_iota(jnp.int32, gp.shape, 1)
    gates = jnp.where(lane < NH_M, gp, _log_sigmoid(gp))
    gates_t = gates.T

    r_i = lax.broadcasted_iota(jnp.int32, (chunk, chunk), 0)
    c_i = lax.broadcasted_iota(jnp.int32, (chunk, chunk), 1)
    causal = r_i >= c_i
    tril = causal.astype(F32)
    triu = (r_i <= c_i).astype(F32)

    unit = 0
    for c in range(tm // chunk):
        rows = slice(c * chunk, (c + 1) * chunk)
        g_c = gates[rows, :]
        g_r = gates_t[0:2 * NH_M, rows]
        cum_c = jnp.dot(tril, g_c, precision=lax.Precision.HIGHEST, preferred_element_type=F32)
        cum_r = jnp.dot(g_r, triu, precision=lax.Precision.HIGHEST, preferred_element_type=F32)
        for h in range(NH_M):
            zc = slice(unit * zw, (unit + 1) * zw)
            z_ref[:, zc] = _dot(xn_ref[...], w_ref[:, N_Z_MLSTM * d + unit * zw:
                                                    N_Z_MLSTM * d + (unit + 1) * zw]).astype(BF16)
            unit += 1

            qc = q_ref[rows, h * dqk:(h + 1) * dqk]
            kc = k_ref[rows, h * dqk:(h + 1) * dqk]
            vc = v_ref[rows, h * dv:(h + 1) * dv]
            qb = qc.astype(BF16)
            i_col = g_c[:, h:h + 1]
            b_col = cum_c[:, NH_M + h:NH_M + h + 1]
            i_row = g_r[h:h + 1, :]
            b_row = cum_r[NH_M + h:NH_M + h + 1, :]
            m_prev = m_ref[h, :, 0:1]
            g_tot = b_col[chunk - 1:chunk, :]

            log_d = jnp.where(causal, b_col + (i_row - b_row), -jnp.inf)
            m_inter = b_col + m_prev
            m_t = jnp.maximum(m_inter, jnp.max(log_d, axis=-1, keepdims=True))
            d_mat = jnp.exp(log_d - m_t)
            s = lax.dot_general(qb, kc.astype(BF16), (((1,), (1,)), ((), ())),
                                preferred_element_type=F32) * d_mat
            inter = jnp.exp(m_inter - m_t)
            num = _dot(s.astype(BF16), vc) + inter * _dot(qb, c_ref[h].astype(BF16))
            den = (jnp.sum(s, axis=-1, keepdims=True)
                   + inter * jnp.sum(qc * n_ref[h], axis=-1, keepdims=True))
            hh = num / jnp.maximum(jnp.abs(den), jnp.exp(-m_t))

            a_col = g_tot - b_col + i_col
            m_new = jnp.maximum(g_tot + m_prev, jnp.max(a_col, axis=0, keepdims=True))
            wk = jnp.exp(a_col - m_new) * kc
            decay = jnp.exp(g_tot + m_prev - m_new)
            c_ref[h] = decay * c_ref[h] + lax.dot_general(
                wk.astype(BF16), vc, (((0,), (0,)), ((), ())), preferred_element_type=F32)
            n_ref[h] = decay * n_ref[h] + jnp.sum(wk, axis=0, keepdims=True)
            m_ref[h] = jnp.broadcast_to(m_new, m_ref.shape[1:])

            mu = jnp.mean(hh, axis=-1, keepdims=True)
            hc = hh - mu
            hn = hc * lax.rsqrt(jnp.mean(hc * hc, axis=-1, keepdims=True) + EPS)
            hn = hn * ng_ref[:, h * dv:(h + 1) * dv]
            og = jax.nn.sigmoid(o_ref[rows, h * dv:(h + 1) * dv].astype(F32))
            y_ref[rows, h * dv:(h + 1) * dv] = (hn * og).astype(BF16)


def _mixa(h, b, s, norms, l, w_main, w_if, conv_w, conv_b, b_if, norm_g):
    n, d = h.shape
    dv = d // NH_M
    dqk = dv // 2
    tm = min(MIX_TM, s)
    chunk = min(MLSTM_L, tm)
    nt = s // tm
    assert (N_Z_REST * d) % ((tm // chunk) * NH_M * 128) == 0
    return pl.pallas_call(
        functools.partial(_mixa_body, tm=tm, chunk=chunk, d=d, dqk=dqk, dv=dv),
        grid=(b, nt),
        in_specs=[
            pl.BlockSpec((tm, d), lambda bi, i: (bi * nt + i, 0)),
            _resident((None, 1, d), lambda bi, i: (l * N_NORMS + 2, 0, 0)),
            _resident((None, d, N_Z_BLOCKS * d), lambda bi, i: (l, 0, 0)),
            _resident((None, d, GATE_LANES), lambda bi, i: (l, 0, 0)),
            _resident((None, CONV_K, d), lambda bi, i: (l, 0, 0)),
            _resident((None, 1, d), lambda bi, i: (l, 0, 0)),
            _resident((None, 1, GATE_LANES), lambda bi, i: (l, 0, 0)),
            _resident((None, 1, d), lambda bi, i: (l, 0, 0)),
        ],
        out_specs=[
            pl.BlockSpec((tm, d), lambda bi, i: (bi * nt + i, 0)),
            pl.BlockSpec((tm, N_Z_REST * d), lambda bi, i: (bi * nt + i, 0)),
        ],
        out_shape=[
            jax.ShapeDtypeStruct((n, d), BF16),
            jax.ShapeDtypeStruct((n, N_Z_REST * d), BF16),
        ],
        scratch_shapes=[
            pltpu.VMEM((tm, d), BF16),
            pltpu.VMEM((tm + HIST, d), F32),
            pltpu.VMEM((tm, NH_M * dqk), F32),
            pltpu.VMEM((tm, NH_M * dqk), F32),
            pltpu.VMEM((tm, d), BF16),
            pltpu.VMEM((tm, d), BF16),
            pltpu.VMEM((NH_M, dqk, dv), F32),
            pltpu.VMEM((NH_M, 1, dqk), F32),
            pltpu.VMEM((NH_M, 1, GATE_LANES), F32),
        ],
        compiler_params=_params(("arbitrary", "arbitrary")),
        name="mixa",
    )(h, norms, w_main, w_if, conv_w, conv_b, b_if, norm_g)


def _mixb_body(h_ref, z_ref, ym_ref, kt_ref, v_ref, lng_ref, lnb_ref, ws_ref, bs_ref, wb_ref, wo_ref, g_ref,
               o_ref, ys_ref, yx_ref, *, tm, d):
    dg = d // SGU_GROUPS
    dh = d // NH_X
    zu, zv, zq, zg = 0, d, 2 * d, 3 * d

    def gate(j):
        return jax.nn.sigmoid(z_ref[:, zg + j * d:zg + (j + 1) * d].astype(F32))

    merged = gate(0) * _dot(ym_ref[...], wb_ref[0])

    r_i = lax.broadcasted_iota(jnp.int32, (SGU_CHUNK, SGU_CHUNK), 0)
    c_i = lax.broadcasted_iota(jnp.int32, (SGU_CHUNK, SGU_CHUNK), 1)
    causal = r_i >= c_i
    for g in range(SGU_GROUPS):
        cols = slice(g * dg, (g + 1) * dg)
        v = jax.nn.gelu(z_ref[:, zv + g * dg:zv + (g + 1) * dg].astype(F32))
        mu = jnp.mean(v, axis=-1, keepdims=True)
        vc = v - mu
        vn = vc * lax.rsqrt(jnp.mean(vc * vc, axis=-1, keepdims=True) + EPS) * lng_ref[:, cols]
        vn = (vn + lnb_ref[:, cols]).astype(BF16)
        w = jnp.where(causal, ws_ref[g], 0.0).astype(BF16)
        for c in range(tm // SGU_CHUNK):
            rows = slice(c * SGU_CHUNK, (c + 1) * SGU_CHUNK)
            mixed = _dot(w, vn[rows, :]) + bs_ref[g]
            u = jax.nn.gelu(z_ref[rows, zu + g * dg:zu + (g + 1) * dg].astype(F32))
            ys_ref[rows, cols] = (u * mixed).astype(BF16)
    merged = merged + gate(1) * _dot(ys_ref[...], wb_ref[1])

    for h in range(NH_X):
        cols = slice(h * dh, (h + 1) * dh)
        s = _dot(z_ref[:, zq + h * dh:zq + (h + 1) * dh], kt_ref[cols, :])
        p = jnp.exp(s - jnp.max(s, axis=-1, keepdims=True))
        p = p / jnp.sum(p, axis=-1, keepdims=True)
        yx_ref[:, cols] = _dot(p.astype(BF16), v_ref[:, cols]).astype(BF16)
    merged = merged + gate(2) * _dot(yx_ref[...], wb_ref[2])

    y = _dot(merged.astype(BF16), wo_ref[...])
    o_ref[...] = h_ref[...] + _rms(y, g_ref[...])


def _mixb(h, z, y_m, kt, v, b, s, ln_g, ln_b, w_s, b_s, w_branch, w_out, norms, l):
    n, d = h.shape
    m = v.shape[1]
    tm = min(MIX_TM, s)
    nt = s // tm
    tok = pl.BlockSpec((tm, d), lambda bi, i: (bi * nt + i, 0))
    return pl.pallas_call(
        functools.partial(_mixb_body, tm=tm, d=d),
        grid=(b, nt),
        in_specs=[
            tok,
            pl.BlockSpec((tm, N_Z_REST * d), lambda bi, i: (bi * nt + i, 0)),
            tok,
            pl.BlockSpec((None, d, m), lambda bi, i: (bi, 0, 0)),
            pl.BlockSpec((None, m, d), lambda bi, i: (bi, 0, 0)),
            _resident((None, 1, d), lambda bi, i: (l, 0, 0)),
            _resident((None, 1, d), lambda bi, i: (l, 0, 0)),
            _resident((None, SGU_GROUPS, SGU_CHUNK, SGU_CHUNK), lambda bi, i: (l, 0, 0, 0)),
            _resident((None, SGU_GROUPS, SGU_CHUNK, 1), lambda bi, i: (l, 0, 0, 0)),
            _resident((None, N_BRANCH, d, d), lambda bi, i: (l, 0, 0, 0)),
            _resident((None, d, d), lambda bi, i: (l, 0, 0)),
            _resident((None, 1, d), lambda bi, i: (l * N_NORMS + 3, 0, 0)),
        ],
        out_specs=tok,
        out_shape=jax.ShapeDtypeStruct((n, d), F32),
        scratch_shapes=[pltpu.VMEM((tm, d), BF16), pltpu.VMEM((tm, d), BF16)],
        compiler_params=_params(("parallel", "parallel")),
        name="mixb",
    )(h, z, y_m, kt, v, ln_g, ln_b, w_s, b_s, w_branch, w_out, norms)


def kernel(x, mem, norm_g, w_ffn_gu, w_ffn_down, w_in, conv_w, conv_b, b_i, b_f, mlstm_norm_g,
           sgu_ln_g, sgu_ln_b, sgu_w_s, sgu_b_s, w_mkv, w_branch, w_out):
    b, s, d = x.shape
    depth = norm_g.shape[0]
    dff = w_ffn_down.shape[2]
    n = b * s
    assert s % MIX_TM == 0 or s < MIX_TM
    assert dff % FFN_CW == 0

    norms = norm_g.reshape(depth * N_NORMS, 1, d)
    wgu = w_ffn_gu.astype(BF16)
    wd = w_ffn_down.astype(BF16)
    n_gate = 2 * NH_M
    c_if = N_Z_MLSTM * d
    w_in_bf = w_in.astype(BF16)
    w_main = jnp.concatenate([w_in_bf[:, :, :c_if], w_in_bf[:, :, c_if + n_gate:]], axis=-1)
    w_if = w_in_bf[:, :, c_if:c_if + GATE_LANES]
    b_if = jnp.pad(jnp.concatenate([b_i, b_f], axis=-1), ((0, 0), (0, GATE_LANES - n_gate)))
    b_if = b_if.reshape(depth, 1, GATE_LANES)
    wkv = w_mkv.astype(BF16)
    wb = w_branch.astype(BF16)
    wo = w_out.astype(BF16)
    conv_b3 = conv_b.reshape(depth, 1, d)
    ng3 = mlstm_norm_g.reshape(depth, 1, d)
    lng3 = sgu_ln_g.reshape(depth, 1, d)
    lnb3 = sgu_ln_b.reshape(depth, 1, d)
    bs4 = sgu_b_s.reshape(depth, SGU_GROUPS, SGU_CHUNK, 1)

    h = x.reshape(n, d)
    for l in range(depth):
        h = _ffn(h, norms, l, 0, 1, wgu, wd, 0)
        kt, v = _memkv(mem, norms, l, wkv)
        y_m, z = _mixa(h, b, s, norms, l, w_main, w_if, conv_w, conv_b3, b_if, ng3)
        h = _mixb(h, z, y_m, kt, v, b, s, lng3, lnb3, sgu_w_s, bs4, wb, wo, norms, l)
        h = _ffn(h, norms, l, 4, 5, wgu, wd, 1)
    return h.reshape(b, s, d)
```

```python
import functools

import jax
import jax.numpy as jnp
from jax import lax
from jax.experimental import pallas as pl
from jax.experimental.pallas import tpu as pltpu

F32 = jnp.float32
BF16 = jnp.bfloat16

EPS = 1e-6
NEG_INIT = -1e30
NH_M = 4
NH_X = 4
SGU_GROUPS = 4
SGU_CHUNK = 128
CONV_K = 4
N_BRANCH = 3
N_NORMS = 7
N_Z_MLSTM = 3
N_Z_REST = 6
GATE_LANES = 128
HIST = 8

VMEM_LIMIT_V7X = 56 * 1024 * 1024

FFN_TM = 1024
FFN_CW = 256
MIX_TM = 512
MLSTM_L = 256
MIXB_SUB = 1


def _params(semantics):
    return pltpu.CompilerParams(dimension_semantics=semantics, vmem_limit_bytes=VMEM_LIMIT_V7X)


def _resident(block_shape, index_map):
    return pl.BlockSpec(block_shape, index_map, pipeline_mode=pl.Buffered(1))


def _rms(x, g):
    return x * lax.rsqrt(jnp.mean(x * x, axis=-1, keepdims=True) + EPS) * g


def _dot(a, b):
    return jnp.dot(a, b, preferred_element_type=F32)


LOG2E = 1.4426950408889634
GELU_K = 2.0 * 0.7978845608028654
GELU_C = 0.044715


def _sigmoid(x):
    return 1.0 / (1.0 + jnp.exp2(x * (-LOG2E)))


def _gelu_tanh(x):
    t = x * ((-GELU_K * LOG2E) + (-GELU_K * GELU_C * LOG2E) * (x * x))
    return x / (1.0 + jnp.exp2(t))


def _ffn_body(x_ref, gpre_ref, gpost_ref, wgu_ref, wd_ref, o_ref, hid_ref, *, dff, cw):
    x = x_ref[...]
    xn = _rms(x, gpre_ref[...]).astype(BF16)
    for c in range(dff // cw):
        a = _dot(xn, wgu_ref[:, c * cw:(c + 1) * cw])
        b = _dot(xn, wgu_ref[:, dff + c * cw:dff + (c + 1) * cw])
        hid_ref[:, c * cw:(c + 1) * cw] = (a * jax.nn.sigmoid(a) * b).astype(BF16)
    f = _dot(hid_ref[...], wd_ref[...])
    o_ref[...] = x + 0.5 * _rms(f, gpost_ref[...])


def _ffn(h, norms, l, k_pre, k_post, wgu, wd, ffn_idx):
    n, d = h.shape
    dff = wd.shape[2]
    tm = min(FFN_TM, n)
    return pl.pallas_call(
        functools.partial(_ffn_body, dff=dff, cw=FFN_CW),
        grid=(n // tm,),
        in_specs=[
            pl.BlockSpec((tm, d), lambda i: (i, 0)),
            _resident((None, 1, d), lambda i: (l * N_NORMS + k_pre, 0, 0)),
            _resident((None, 1, d), lambda i: (l * N_NORMS + k_post, 0, 0)),
            _resident((None, None, d, 2 * dff), lambda i: (l, ffn_idx, 0, 0)),
            _resident((None, None, dff, d), lambda i: (l, ffn_idx, 0, 0)),
        ],
        out_specs=pl.BlockSpec((tm, d), lambda i: (i, 0)),
        out_shape=jax.ShapeDtypeStruct((n, d), F32),
        scratch_shapes=[pltpu.VMEM((tm, dff), BF16)],
        compiler_params=_params(("parallel",)),
        name="ffn",
    )(h, norms, norms, wgu, wd)


def _memkv_body(mem_ref, g_ref, wkv_ref, kt_ref, v_ref, *, d, scale):
    mn = _rms(mem_ref[...], g_ref[...]).astype(BF16)
    k = _dot(mn, wkv_ref[:, :d]) * scale
    kt_ref[...] = k.T.astype(BF16)
    v_ref[...] = _dot(mn, wkv_ref[:, d:]).astype(BF16)


def _memkv(mem, norms, l, wkv):
    b, m, d = mem.shape
    scale = float((d // NH_X) ** -0.5)
    return pl.pallas_call(
        functools.partial(_memkv_body, d=d, scale=scale),
        grid=(b,),
        in_specs=[
            pl.BlockSpec((None, m, d), lambda i: (i, 0, 0)),
            _resident((None, 1, d), lambda i: (l * N_NORMS + 6, 0, 0)),
            _resident((None, d, 2 * d), lambda i: (l, 0, 0)),
        ],
        out_specs=[
            pl.BlockSpec((None, d, m), lambda i: (i, 0, 0)),
            pl.BlockSpec((None, m, d), lambda i: (i, 0, 0)),
        ],
        out_shape=[
            jax.ShapeDtypeStruct((b, d, m), BF16),
            jax.ShapeDtypeStruct((b, m, d), BF16),
        ],
        compiler_params=_params(("parallel",)),
        name="memkv",
    )(mem, norms, wkv)


def _log_sigmoid(x):
    return -(jnp.maximum(-x, 0.0) + jnp.log1p(jnp.exp(-jnp.abs(x))))


def _split3(x):
    hi = x.astype(BF16)
    r1 = x - hi.astype(F32)
    mid = r1.astype(BF16)
    lo = (r1 - mid.astype(F32)).astype(BF16)
    return hi, mid, lo


def _mixa_body(h_ref, g_ref, wm_ref, wr_ref, wif_ref, cw_ref, cb_ref, bif_ref, ng_ref,
               y_ref, z_ref,
               xn_ref, xs_ref, q_ref, k_ref, v_ref, o_ref, gc_ref, gr_ref, c_ref, n_ref, m_ref,
               *, tm, chunk, d, dqk, dv):
    i = pl.program_id(1)
    nqk = NH_M * dqk
    n_chunks = tm // chunk
    zw = wr_ref.shape[-1] // NH_M

    @pl.when(i == 0)
    def _():
        c_ref[...] = jnp.zeros_like(c_ref)
        n_ref[...] = jnp.zeros_like(n_ref)
        m_ref[...] = jnp.full_like(m_ref, NEG_INIT)
        xs_ref[0:HIST, :] = jnp.zeros((HIST, d), F32)

    @pl.when(i > 0)
    def _():
        xs_ref[0:HIST, :] = xs_ref[tm:tm + HIST, :]

    xn_ref[...] = _rms(h_ref[...], g_ref[...]).astype(BF16)
    xs_ref[HIST:, :] = _dot(xn_ref[...], wm_ref[:, 0:d])
    gp = _dot(xn_ref[...], wif_ref[...]) + bif_ref[...]
    v_ref[...] = _dot(xn_ref[...], wm_ref[:, d:2 * d]).astype(BF16)
    o_ref[...] = _dot(xn_ref[...], wm_ref[:, 2 * d:3 * d]).astype(BF16)

    lane = lax.broadcasted_iota(jnp.int32, gp.shape, 1)
    gates = jnp.where(lane < NH_M, gp, _log_sigmoid(gp))
    gc_ref[...] = gates
    gates_t = gates.T
    for c in range(n_chunks):
        gr_ref[c] = gates_t[0:2 * NH_M, c * chunk:(c + 1) * chunk]

    r_i = lax.broadcasted_iota(jnp.int32, (chunk, chunk), 0)
    c_i = lax.broadcasted_iota(jnp.int32, (chunk, chunk), 1)
    causal = r_i >= c_i

    def chunk_body(c):
        rows = slice(c * chunk, (c + 1) * chunk)

        conv = cb_ref[...]
        for j in range(CONV_K):
            conv = conv + cw_ref[j:j + 1, :] * xs_ref[pl.ds(c * chunk + HIST - (CONV_K - 1) + j, chunk), :]
        qk = conv * _sigmoid(conv)
        q_ref[rows, :] = qk[:, :nqk]
        k_ref[rows, :] = qk[:, nqk:] * float(dqk ** -0.5)

        g_c = gc_ref[rows, :]
        g_r = gr_ref[c]
        tril = jnp.where(causal, 1.0, 0.0).astype(BF16)
        triu = jnp.where(r_i <= c_i, 1.0, 0.0).astype(BF16)
        cum_c = sum(_dot(tril, p) for p in _split3(g_c))
        cum_r = sum(_dot(p, triu) for p in _split3(g_r))
        for h in range(NH_M):
            z_ref[c, :, h * zw:(h + 1) * zw] = _dot(xn_ref[...], wr_ref[c, :, h * zw:(h + 1) * zw]).astype(BF16)

            qc = q_ref[rows, h * dqk:(h + 1) * dqk]
            kc = k_ref[rows, h * dqk:(h + 1) * dqk]
            vc = v_ref[rows, h * dv:(h + 1) * dv]
            qb = qc.astype(BF16)
            i_col = g_c[:, h:h + 1]
            b_col = cum_c[:, NH_M + h:NH_M + h + 1]
            i_row = g_r[h:h + 1, :]
            b_row = cum_r[NH_M + h:NH_M + h + 1, :]
            m_prev = m_ref[h, :, 0:1]
            g_tot = b_col[chunk - 1:chunk, :]

            log_d = jnp.where(causal, b_col + (i_row - b_row), -jnp.inf)
            m_inter = b_col + m_prev
            m_t = jnp.maximum(m_inter, jnp.max(log_d, axis=-1, keepdims=True))
            d_mat = jnp.exp(log_d - m_t)
            s = lax.dot_general(qb, kc.astype(BF16), (((1,), (1,)), ((), ())),
                                preferred_element_type=F32) * d_mat
            inter = jnp.exp(m_inter - m_t)
            num = _dot(s.astype(BF16), vc) + inter * _dot(qb, c_ref[h].astype(BF16))
            den = (jnp.sum(s, axis=-1, keepdims=True)
                   + inter * jnp.sum(qc * n_ref[h], axis=-1, keepdims=True))
            hh = num / jnp.maximum(jnp.abs(den), jnp.exp(-m_t))

            a_col = g_tot - b_col + i_col
            m_new = jnp.maximum(g_tot + m_prev, jnp.max(a_col, axis=0, keepdims=True))
            wk = jnp.exp(a_col - m_new) * kc
            decay = jnp.exp(g_tot + m_prev - m_new)
            c_ref[h] = decay * c_ref[h] + lax.dot_general(
                wk.astype(BF16), vc, (((0,), (0,)), ((), ())), preferred_element_type=F32)
            n_ref[h] = decay * n_ref[h] + jnp.sum(wk, axis=0, keepdims=True)
            m_ref[h] = jnp.broadcast_to(m_new, m_ref.shape[1:])

            mu = jnp.mean(hh, axis=-1, keepdims=True)
            hc = hh - mu
            hn = hc * lax.rsqrt(jnp.mean(hc * hc, axis=-1, keepdims=True) + EPS)
            hn = hn * ng_ref[:, h * dv:(h + 1) * dv]
            og = _sigmoid(o_ref[rows, h * dv:(h + 1) * dv].astype(F32))
            y_ref[rows, h * dv:(h + 1) * dv] = (hn * og).astype(BF16)

    for c in range(n_chunks):
        pl.when(i >= -c)(functools.partial(chunk_body, c))


def _mixa(h, b, s, norms, l, w_mlstm, w_rest, w_if, conv_w, conv_b, b_if, norm_g):
    n, d = h.shape
    dv = d // NH_M
    dqk = dv // 2
    tm = min(MIX_TM, s)
    chunk = min(MLSTM_L, tm)
    nt = s // tm
    n_chunks, zc = w_rest.shape[1], w_rest.shape[3]
    assert n_chunks == tm // chunk and zc % (NH_M * 128) == 0
    return pl.pallas_call(
        functools.partial(_mixa_body, tm=tm, chunk=chunk, d=d, dqk=dqk, dv=dv),
        grid=(b, nt),
        in_specs=[
            pl.BlockSpec((tm, d), lambda bi, i: (bi * nt + i, 0)),
            _resident((None, 1, d), lambda bi, i: (l * N_NORMS + 2, 0, 0)),
            _resident((None, d, N_Z_MLSTM * d), lambda bi, i: (l, 0, 0)),
            _resident((None, n_chunks, d, zc), lambda bi, i: (l, 0, 0, 0)),
            _resident((None, d, GATE_LANES), lambda bi, i: (l, 0, 0)),
            _resident((None, CONV_K, d), lambda bi, i: (l, 0, 0)),
            _resident((None, 1, d), lambda bi, i: (l, 0, 0)),
            _resident((None, 1, GATE_LANES), lambda bi, i: (l, 0, 0)),
            _resident((None, 1, d), lambda bi, i: (l, 0, 0)),
        ],
        out_specs=[
            pl.BlockSpec((tm, d), lambda bi, i: (bi * nt + i, 0)),
            pl.BlockSpec((n_chunks, tm, zc), lambda bi, i: (0, bi * nt + i, 0)),
        ],
        out_shape=[
            jax.ShapeDtypeStruct((n, d), BF16),
            jax.ShapeDtypeStruct((n_chunks, n, zc), BF16),
        ],
        scratch_shapes=[
            pltpu.VMEM((tm, d), BF16),
            pltpu.VMEM((tm + HIST, d), F32),
            pltpu.VMEM((tm, NH_M * dqk), F32),
            pltpu.VMEM((tm, NH_M * dqk), F32),
            pltpu.VMEM((tm, d), BF16),
            pltpu.VMEM((tm, d), BF16),
            pltpu.VMEM((tm, GATE_LANES), F32),
            pltpu.VMEM((n_chunks, 2 * NH_M, chunk), F32),
            pltpu.VMEM((NH_M, dqk, dv), F32),
            pltpu.VMEM((NH_M, 1, dqk), F32),
            pltpu.VMEM((NH_M, 1, GATE_LANES), F32),
        ],
        compiler_params=_params(("arbitrary", "arbitrary")),
        name="mixa",
    )(h, norms, w_mlstm, w_rest, w_if, conv_w, conv_b, b_if, norm_g)


def _mixb_body(h_ref, z_ref, ym_ref, kt_ref, v_ref, lng_ref, lnb_ref, ws_ref, bs_ref, wb_ref, wo_ref, g_ref,
               o_ref, ys_ref, yx_ref, *, tm, d, n_sub):
    dg = d // SGU_GROUPS
    dh = d // NH_X
    zu, zv, zq, zg = 0, d, 2 * d, 3 * d
    zc = z_ref.shape[-1]

    def zcols(rows, col0, width):
        part, off = divmod(col0, zc)
        assert off + width <= zc
        return z_ref[part, rows, off:off + width]

    r_i = lax.broadcasted_iota(jnp.int32, (SGU_CHUNK, SGU_CHUNK), 0)
    c_i = lax.broadcasted_iota(jnp.int32, (SGU_CHUNK, SGU_CHUNK), 1)
    causal = r_i >= c_i
    w_sgu = [jnp.where(causal, ws_ref[g], 0.0).astype(BF16) for g in range(SGU_GROUPS)]

    rs = tm // n_sub
    for sub in range(n_sub):
        r0 = sub * rs
        srows = slice(r0, r0 + rs)

        def gate(j):
            return _sigmoid(zcols(srows, zg + j * d, d).astype(F32))

        merged = gate(0) * _dot(ym_ref[srows, :], wb_ref[0])

        for g in range(SGU_GROUPS):
            cols = slice(g * dg, (g + 1) * dg)
            v = _gelu_tanh(zcols(srows, zv + g * dg, dg).astype(F32))
            mu = jnp.mean(v, axis=-1, keepdims=True)
            vc = v - mu
            vn = vc * lax.rsqrt(jnp.mean(vc * vc, axis=-1, keepdims=True) + EPS) * lng_ref[:, cols]
            vn = (vn + lnb_ref[:, cols]).astype(BF16)
            for c in range(rs // SGU_CHUNK):
                rows = slice(r0 + c * SGU_CHUNK, r0 + (c + 1) * SGU_CHUNK)
                mixed = _dot(w_sgu[g], vn[c * SGU_CHUNK:(c + 1) * SGU_CHUNK, :]) + bs_ref[g]
                u = _gelu_tanh(zcols(rows, zu + g * dg, dg).astype(F32))
                ys_ref[rows, cols] = (u * mixed).astype(BF16)
        merged = merged + gate(1) * _dot(ys_ref[srows, :], wb_ref[1])

        for h in range(NH_X):
            cols = slice(h * dh, (h + 1) * dh)
            s = _dot(zcols(srows, zq + h * dh, dh), kt_ref[cols, :])
            p = jnp.exp(s - jnp.max(s, axis=-1, keepdims=True))
            p = p / jnp.sum(p, axis=-1, keepdims=True)
            yx_ref[srows, cols] = _dot(p.astype(BF16), v_ref[:, cols]).astype(BF16)
        merged = merged + gate(2) * _dot(yx_ref[srows, :], wb_ref[2])

        y = _dot(merged.astype(BF16), wo_ref[...])
        o_ref[srows, :] = h_ref[srows, :] + _rms(y, g_ref[...])


def _mixb(h, z, y_m, kt, v, b, s, ln_g, ln_b, w_s, b_s, w_branch, w_out, norms, l):
    n, d = h.shape
    m = v.shape[1]
    tm = min(MIX_TM, s)
    nt = s // tm
    tok = pl.BlockSpec((tm, d), lambda bi, i: (bi * nt + i, 0))
    return pl.pallas_call(
        functools.partial(_mixb_body, tm=tm, d=d, n_sub=MIXB_SUB),
        grid=(b, nt),
        in_specs=[
            tok,
            pl.BlockSpec((z.shape[0], tm, z.shape[2]), lambda bi, i: (0, bi * nt + i, 0)),
            tok,
            pl.BlockSpec((None, d, m), lambda bi, i: (bi, 0, 0)),
            pl.BlockSpec((None, m, d), lambda bi, i: (bi, 0, 0)),
            _resident((None, 1, d), lambda bi, i: (l, 0, 0)),
            _resident((None, 1, d), lambda bi, i: (l, 0, 0)),
            _resident((None, SGU_GROUPS, SGU_CHUNK, SGU_CHUNK), lambda bi, i: (l, 0, 0, 0)),
            _resident((None, SGU_GROUPS, SGU_CHUNK, 1), lambda bi, i: (l, 0, 0, 0)),
            _resident((None, N_BRANCH, d, d), lambda bi, i: (l, 0, 0, 0)),
            _resident((None, d, d), lambda bi, i: (l, 0, 0)),
            _resident((None, 1, d), lambda bi, i: (l * N_NORMS + 3, 0, 0)),
        ],
        out_specs=tok,
        out_shape=jax.ShapeDtypeStruct((n, d), F32),
        scratch_shapes=[pltpu.VMEM((tm, d), BF16), pltpu.VMEM((tm, d), BF16)],
        compiler_params=_params(("parallel", "parallel")),
        name="mixb",
    )(h, z, y_m, kt, v, ln_g, ln_b, w_s, b_s, w_branch, w_out, norms)


def kernel(x, mem, norm_g, w_ffn_gu, w_ffn_down, w_in, conv_w, conv_b, b_i, b_f, mlstm_norm_g,
           sgu_ln_g, sgu_ln_b, sgu_w_s, sgu_b_s, w_mkv, w_branch, w_out):
    b, s, d = x.shape
    depth = norm_g.shape[0]
    dff = w_ffn_down.shape[2]
    n = b * s
    assert s % MIX_TM == 0 or s < MIX_TM
    assert dff % FFN_CW == 0

    norms = norm_g.reshape(depth * N_NORMS, 1, d)
    wgu = w_ffn_gu.astype(BF16)
    wd = w_ffn_down.astype(BF16)
    n_gate = 2 * NH_M
    c_if = N_Z_MLSTM * d
    w_in_bf = w_in.astype(BF16)
    w_mlstm = w_in_bf[:, :, :c_if]
    n_chunks = max(min(MIX_TM, s) // MLSTM_L, 1)
    zc = N_Z_REST * d // n_chunks
    w_rest = jnp.stack([w_in_bf[:, :, c_if + n_gate + c * zc:c_if + n_gate + (c + 1) * zc]
                        for c in range(n_chunks)], axis=1)
    w_if = w_in_bf[:, :, c_if:c_if + GATE_LANES]
    b_if = jnp.pad(jnp.concatenate([b_i, b_f], axis=-1), ((0, 0), (0, GATE_LANES - n_gate)))
    b_if = b_if.reshape(depth, 1, GATE_LANES)
    wkv = w_mkv.astype(BF16)
    wb = w_branch.astype(BF16)
    wo = w_out.astype(BF16)
    conv_b3 = conv_b.reshape(depth, 1, d)
    ng3 = mlstm_norm_g.reshape(depth, 1, d)
    lng3 = sgu_ln_g.reshape(depth, 1, d)
    lnb3 = sgu_ln_b.reshape(depth, 1, d)
    bs4 = sgu_b_s.reshape(depth, SGU_GROUPS, SGU_CHUNK, 1)

    h = x.reshape(n, d)
    for l in range(depth):
        h = _ffn(h, norms, l, 0, 1, wgu, wd, 0)
        kt, v = _memkv(mem, norms, l, wkv)
        y_m, z = _mixa(h, b, s, norms, l, w_mlstm, w_rest, w_if, conv_w, conv_b3, b_if, ng3)
        h = _mixb(h, z, y_m, kt, v, b, s, lng3, lnb3, sgu_w_s, bs4, wb, wo, norms, l)
        h = _ffn(h, norms, l, 4, 5, wgu, wd, 1)
    return h.reshape(b, s, d)
```

```python
import functools

import jax
import jax.numpy as jnp
from jax import lax
from jax.experimental import pallas as pl
from jax.experimental.pallas import tpu as pltpu

F32 = jnp.float32
BF16 = jnp.bfloat16

EPS = 1e-6
NEG_INIT = -1e30
NH_M = 4
NH_X = 4
SGU_GROUPS = 4
SGU_CHUNK = 128
CONV_K = 4
N_BRANCH = 3
N_NORMS = 7
N_Z_MLSTM = 3
N_Z_REST = 6
GATE_LANES = 128
HIST = 8

VMEM_LIMIT_V7X = 56 * 1024 * 1024

FFN_TM = 1024
FFN_RS = 512
FFN_CW = 256
MIX_TM = 512
MLSTM_L = 256


def _params(semantics):
    return pltpu.CompilerParams(dimension_semantics=semantics, vmem_limit_bytes=VMEM_LIMIT_V7X)


def _resident(block_shape, index_map):
    return pl.BlockSpec(block_shape, index_map, pipeline_mode=pl.Buffered(1))


def _rms(x, g):
    return x * lax.rsqrt(jnp.mean(x * x, axis=-1, keepdims=True) + EPS) * g


def _dot(a, b):
    return jnp.dot(a, b, preferred_element_type=F32)


LOG2E = 1.4426950408889634
GELU_K = 2.0 * 0.7978845608028654
GELU_C = 0.044715


def _sigmoid(x):
    return 1.0 / (1.0 + jnp.exp2(x * (-LOG2E)))


def _gelu_tanh(x):
    t = x * ((-GELU_K * LOG2E) + (-GELU_K * GELU_C * LOG2E) * (x * x))
    return x / (1.0 + jnp.exp2(t))


def _ffn_body(x_ref, gpre_ref, gpost_ref, wgu_ref, wd_ref, o_ref, xn_ref, hid_ref, *, dff, cw, rs):
    n_sub = x_ref.shape[0] // rs
    nch = dff // cw

    def rows(r):
        return slice(r * rs, (r + 1) * rs)

    def normalise(r):
        xn_ref[rows(r), :] = _rms(x_ref[rows(r), :], gpre_ref[...]).astype(BF16)

    def hidden_chunk(r, c):
        xn = xn_ref[rows(r), :]
        a = _dot(xn, wgu_ref[:, c * cw:(c + 1) * cw])
        b = _dot(xn, wgu_ref[:, dff + c * cw:dff + (c + 1) * cw])
        hid_ref[rows(r), c * cw:(c + 1) * cw] = (a * jax.nn.sigmoid(a) * b).astype(BF16)

    def finish(r):
        f = _dot(hid_ref[rows(r), :], wd_ref[...])
        o_ref[rows(r), :] = x_ref[rows(r), :] + 0.5 * _rms(f, gpost_ref[...])

    normalise(0)
    for r in range(n_sub):
        for c in range(nch):
            hidden_chunk(r, c)
            if c == 1 and r + 1 < n_sub:
                normalise(r + 1)
            if c == 1 and r > 0:
                finish(r - 1)
    finish(n_sub - 1)


def _ffn(h, norms, l, k_pre, k_post, wgu, wd, ffn_idx):
    n, d = h.shape
    dff = wd.shape[2]
    tm = min(FFN_TM, n)
    return pl.pallas_call(
        functools.partial(_ffn_body, dff=dff, cw=FFN_CW, rs=min(FFN_RS, tm)),
        grid=(n // tm,),
        in_specs=[
            pl.BlockSpec((tm, d), lambda i: (i, 0)),
            _resident((None, 1, d), lambda i: (l * N_NORMS + k_pre, 0, 0)),
            _resident((None, 1, d), lambda i: (l * N_NORMS + k_post, 0, 0)),
            _resident((None, None, d, 2 * dff), lambda i: (l, ffn_idx, 0, 0)),
            _resident((None, None, dff, d), lambda i: (l, ffn_idx, 0, 0)),
        ],
        out_specs=pl.BlockSpec((tm, d), lambda i: (i, 0)),
        out_shape=jax.ShapeDtypeStruct((n, d), F32),
        scratch_shapes=[pltpu.VMEM((tm, d), BF16), pltpu.VMEM((tm, dff), BF16)],
        compiler_params=_params(("parallel",)),
        name="ffn",
    )(h, norms, norms, wgu, wd)


def _memkv_body(mem_ref, g_ref, wkv_ref, kt_ref, v_ref, *, d, scale):
    mn = _rms(mem_ref[...], g_ref[...]).astype(BF16)
    k = _dot(mn, wkv_ref[:, :d]) * scale
    kt_ref[...] = k.T.astype(BF16)
    v_ref[...] = _dot(mn, wkv_ref[:, d:]).astype(BF16)


def _memkv(mem, norms, l, wkv):
    b, m, d = mem.shape
    scale = float((d // NH_X) ** -0.5)
    return pl.pallas_call(
        functools.partial(_memkv_body, d=d, scale=scale),
        grid=(b,),
        in_specs=[
            pl.BlockSpec((None, m, d), lambda i: (i, 0, 0)),
            _resident((None, 1, d), lambda i: (l * N_NORMS + 6, 0, 0)),
            _resident((None, d, 2 * d), lambda i: (l, 0, 0)),
        ],
        out_specs=[
            pl.BlockSpec((None, d, m), lambda i: (i, 0, 0)),
            pl.BlockSpec((None, m, d), lambda i: (i, 0, 0)),
        ],
        out_shape=[
            jax.ShapeDtypeStruct((b, d, m), BF16),
            jax.ShapeDtypeStruct((b, m, d), BF16),
        ],
        compiler_params=_params(("parallel",)),
        name="memkv",
    )(mem, norms, wkv)


def _log_sigmoid(x):
    return -(jnp.maximum(-x, 0.0) + jnp.log1p(jnp.exp(-jnp.abs(x))))


def _split3(x):
    hi = x.astype(BF16)
    r1 = x - hi.astype(F32)
    mid = r1.astype(BF16)
    lo = (r1 - mid.astype(F32)).astype(BF16)
    return hi, mid, lo


def _mixa_body(h_ref, g_ref, wm_ref, wr_ref, wif_ref, cw_ref, cb_ref, bif_ref, ng_ref,
               y_ref, z_ref,
               xn_ref, xs_ref, q_ref, k_ref, v_ref, o_ref, gc_ref, gr_ref, c_ref, n_ref, m_ref,
               *, tm, chunk, d, dqk, dv):
    i = pl.program_id(1)
    nqk = NH_M * dqk
    n_chunks = tm // chunk
    zc = wr_ref.shape[-1] // n_chunks
    zw = zc // NH_M

    @pl.when(i == 0)
    def _():
        c_ref[...] = jnp.zeros_like(c_ref)
        n_ref[...] = jnp.zeros_like(n_ref)
        m_ref[...] = jnp.full_like(m_ref, NEG_INIT)
        xs_ref[0:HIST, :] = jnp.zeros((HIST, d), F32)

    @pl.when(i > 0)
    def _():
        xs_ref[0:HIST, :] = xs_ref[tm:tm + HIST, :]

    xn_ref[...] = _rms(h_ref[...], g_ref[...]).astype(BF16)
    xs_ref[HIST:, :] = _dot(xn_ref[...], wm_ref[:, 0:d])
    gp = _dot(xn_ref[...], wif_ref[...]) + bif_ref[...]
    v_ref[...] = _dot(xn_ref[...], wm_ref[:, d:2 * d]).astype(BF16)
    o_ref[...] = _dot(xn_ref[...], wm_ref[:, 2 * d:3 * d]).astype(BF16)

    lane = lax.broadcasted_iota(jnp.int32, gp.shape, 1)
    gates = jnp.where(lane < NH_M, gp, _log_sigmoid(gp))
    gc_ref[...] = gates
    gates_t = gates.T
    for c in range(n_chunks):
        gr_ref[c] = gates_t[0:2 * NH_M, c * chunk:(c + 1) * chunk]

    r_i = lax.broadcasted_iota(jnp.int32, (chunk, chunk), 0)
    c_i = lax.broadcasted_iota(jnp.int32, (chunk, chunk), 1)
    causal = r_i >= c_i

    def chunk_body(c):
        rows = slice(c * chunk, (c + 1) * chunk)

        conv = cb_ref[...]
        for j in range(CONV_K):
            conv = conv + cw_ref[j:j + 1, :] * xs_ref[pl.ds(c * chunk + HIST - (CONV_K - 1) + j, chunk), :]
        qk = conv * _sigmoid(conv)
        q_ref[rows, :] = qk[:, :nqk]
        k_ref[rows, :] = qk[:, nqk:] * float(dqk ** -0.5)

        g_c = gc_ref[rows, :]
        g_r = gr_ref[c]
        tril = jnp.where(causal, 1.0, 0.0).astype(BF16)
        triu = jnp.where(r_i <= c_i, 1.0, 0.0).astype(BF16)
        cum_c = sum(_dot(tril, p) for p in _split3(g_c))
        cum_r = sum(_dot(p, triu) for p in _split3(g_r))
        for h in range(NH_M):
            z_ref[c, :, h * zw:(h + 1) * zw] = _dot(
                xn_ref[...], wr_ref[:, c * zc + h * zw:c * zc + (h + 1) * zw]).astype(BF16)

            qc = q_ref[rows, h * dqk:(h + 1) * dqk]
            kc = k_ref[rows, h * dqk:(h + 1) * dqk]
            vc = v_ref[rows, h * dv:(h + 1) * dv]
            qb = qc.astype(BF16)
            i_col = g_c[:, h:h + 1]
            b_col = cum_c[:, NH_M + h:NH_M + h + 1]
            i_row = g_r[h:h + 1, :]
            b_row = cum_r[NH_M + h:NH_M + h + 1, :]
            m_prev = m_ref[h, :, 0:1]
            g_tot = b_col[chunk - 1:chunk, :]

            log_d = jnp.where(causal, b_col + (i_row - b_row), -jnp.inf)
            m_inter = b_col + m_prev
            m_t = jnp.maximum(m_inter, jnp.max(log_d, axis=-1, keepdims=True))
            d_mat = jnp.exp(log_d - m_t)
            s = lax.dot_general(qb, kc.astype(BF16), (((1,), (1,)), ((), ())),
                                preferred_element_type=F32) * d_mat
            inter = jnp.exp(m_inter - m_t)
            num = _dot(s.astype(BF16), vc) + inter * _dot(qb, c_ref[h].astype(BF16))
            den = (jnp.sum(s, axis=-1, keepdims=True)
                   + inter * jnp.sum(qc * n_ref[h], axis=-1, keepdims=True))
            hh = num / jnp.maximum(jnp.abs(den), jnp.exp(-m_t))

            a_col = g_tot - b_col + i_col
            m_new = jnp.maximum(g_tot + m_prev, jnp.max(a_col, axis=0, keepdims=True))
            wk = jnp.exp(a_col - m_new) * kc
            decay = jnp.exp(g_tot + m_prev - m_new)
            c_ref[h] = decay * c_ref[h] + lax.dot_general(
                wk.astype(BF16), vc, (((0,), (0,)), ((), ())), preferred_element_type=F32)
            n_ref[h] = decay * n_ref[h] + jnp.sum(wk, axis=0, keepdims=True)
            m_ref[h] = jnp.broadcast_to(m_new, m_ref.shape[1:])

            mu = jnp.mean(hh, axis=-1, keepdims=True)
            hc = hh - mu
            hn = hc * lax.rsqrt(jnp.mean(hc * hc, axis=-1, keepdims=True) + EPS)
            hn = hn * ng_ref[:, h * dv:(h + 1) * dv]
            og = _sigmoid(o_ref[rows, h * dv:(h + 1) * dv].astype(F32))
            y_ref[rows, h * dv:(h + 1) * dv] = (hn * og).astype(BF16)

    for c in range(n_chunks):
        pl.when(i >= -c)(functools.partial(chunk_body, c))


def _mixa(h, b, s, norms, l, w_mlstm, w_rest, w_if, conv_w, conv_b, b_if, norm_g):
    n, d = h.shape
    dv = d // NH_M
    dqk = dv // 2
    tm = min(MIX_TM, s)
    chunk = min(MLSTM_L, tm)
    nt = s // tm
    n_chunks = tm // chunk
    zc = w_rest.shape[2] // n_chunks
    assert zc * n_chunks == w_rest.shape[2] and zc % (NH_M * 128) == 0
    return pl.pallas_call(
        functools.partial(_mixa_body, tm=tm, chunk=chunk, d=d, dqk=dqk, dv=dv),
        grid=(b, nt),
        in_specs=[
            pl.BlockSpec((tm, d), lambda bi, i: (bi * nt + i, 0)),
            _resident((None, 1, d), lambda bi, i: (l * N_NORMS + 2, 0, 0)),
            _resident((None, d, N_Z_MLSTM * d), lambda bi, i: (l, 0, 0)),
            _resident((None, d, n_chunks * zc), lambda bi, i: (l, 0, 0)),
            _resident((None, d, GATE_LANES), lambda bi, i: (l, 0, 0)),
            _resident((None, CONV_K, d), lambda bi, i: (l, 0, 0)),
            _resident((None, 1, d), lambda bi, i: (l, 0, 0)),
            _resident((None, 1, GATE_LANES), lambda bi, i: (l, 0, 0)),
            _resident((None, 1, d), lambda bi, i: (l, 0, 0)),
        ],
        out_specs=[
            pl.BlockSpec((tm, d), lambda bi, i: (bi * nt + i, 0)),
            pl.BlockSpec((n_chunks, tm, zc), lambda bi, i: (0, bi * nt + i, 0)),
        ],
        out_shape=[
            jax.ShapeDtypeStruct((n, d), BF16),
            jax.ShapeDtypeStruct((n_chunks, n, zc), BF16),
        ],
        scratch_shapes=[
            pltpu.VMEM((tm, d), BF16),
            pltpu.VMEM((tm + HIST, d), F32),
            pltpu.VMEM((tm, NH_M * dqk), F32),
            pltpu.VMEM((tm, NH_M * dqk), F32),
            pltpu.VMEM((tm, d), BF16),
            pltpu.VMEM((tm, d), BF16),
            pltpu.VMEM((tm, GATE_LANES), F32),
            pltpu.VMEM((n_chunks, 2 * NH_M, chunk), F32),
            pltpu.VMEM((NH_M, dqk, dv), F32),
            pltpu.VMEM((NH_M, 1, dqk), F32),
            pltpu.VMEM((NH_M, 1, GATE_LANES), F32),
        ],
        compiler_params=_params(("arbitrary", "arbitrary")),
        name="mixa",
    )(h, norms, w_mlstm, w_rest, w_if, conv_w, conv_b, b_if, norm_g)


def _mixb_body(zf_ref, kt_ref, v_ref, zg_ref, ym_ref, h_ref, lng_ref, lnb_ref, ws_ref, bs_ref, wb_ref, wo_ref,
               g_ref, o_ref, ys_ref, yx_ref, ysp_ref, yxp_ref, mg_ref, yo_ref, *, tm, d):
    k = pl.program_id(0)
    dg = d // SGU_GROUPS
    dh = d // NH_X
    zu, zv, zq = 0, d, 2 * d

    @pl.when(k == 0)
    def _():
        ys_ref[...] = jnp.zeros((tm, d), BF16)
        yx_ref[...] = jnp.zeros((tm, d), BF16)

    ysp_ref[...] = ys_ref[...]
    yxp_ref[...] = yx_ref[...]

    r_i = lax.broadcasted_iota(jnp.int32, (SGU_CHUNK, SGU_CHUNK), 0)
    c_i = lax.broadcasted_iota(jnp.int32, (SGU_CHUNK, SGU_CHUNK), 1)
    causal = r_i >= c_i

    def sgu_group(g):
        cols = slice(g * dg, (g + 1) * dg)
        v = _gelu_tanh(zf_ref[:, zv + g * dg:zv + (g + 1) * dg].astype(F32))
        mu = jnp.mean(v, axis=-1, keepdims=True)
        vc = v - mu
        vn = vc * lax.rsqrt(jnp.mean(vc * vc, axis=-1, keepdims=True) + EPS) * lng_ref[:, cols]
        vn = (vn + lnb_ref[:, cols]).astype(BF16)
        w = jnp.where(causal, ws_ref[g], 0.0).astype(BF16)
        for c in range(tm // SGU_CHUNK):
            rows = slice(c * SGU_CHUNK, (c + 1) * SGU_CHUNK)
            mixed = _dot(w, vn[rows, :]) + bs_ref[g]
            u = _gelu_tanh(zf_ref[rows, zu + g * dg:zu + (g + 1) * dg].astype(F32))
            ys_ref[rows, cols] = (u * mixed).astype(BF16)

    def xattn_head(h):
        cols = slice(h * dh, (h + 1) * dh)
        s = _dot(zf_ref[:, zq + h * dh:zq + (h + 1) * dh], kt_ref[cols, :])
        p = jnp.exp(s - jnp.max(s, axis=-1, keepdims=True))
        p = p / jnp.sum(p, axis=-1, keepdims=True)
        yx_ref[:, cols] = _dot(p.astype(BF16), v_ref[:, cols]).astype(BF16)

    half = d // 2

    def branch_half(j, y_ref, hf):
        cols = slice(hf * half, (hf + 1) * half)
        gate = _sigmoid(zg_ref[:, j * d + hf * half:j * d + (hf + 1) * half].astype(F32))
        term = gate * _dot(y_ref[...], wb_ref[j, :, cols])
        if j == 0:
            mg_ref[:, cols] = term
        else:
            mg_ref[:, cols] += term

    def out_half(hf):
        cols = slice(hf * half, (hf + 1) * half)
        yo_ref[:, cols] = _dot(mg_ref[...].astype(BF16), wo_ref[:, cols])

    sgu_group(0)
    branch_half(0, ym_ref, 0)
    sgu_group(1)
    branch_half(0, ym_ref, 1)
    sgu_group(2)
    branch_half(1, ysp_ref, 0)
    sgu_group(3)
    branch_half(1, ysp_ref, 1)
    xattn_head(0)
    branch_half(2, yxp_ref, 0)
    xattn_head(1)
    branch_half(2, yxp_ref, 1)
    xattn_head(2)
    out_half(0)
    xattn_head(3)
    out_half(1)
    o_ref[...] = h_ref[...] + _rms(yo_ref[...], g_ref[...])


def _mixb(h, z, y_m, kt, v, s, ln_g, ln_b, w_s, b_s, w_branch, w_out, norms, l):
    n, d = h.shape
    m = v.shape[1]
    tm = min(MIX_TM, s)
    nt = s // tm
    last = n // tm - 1
    assert z.shape[0] == 2 and z.shape[2] == N_BRANCH * d

    def cur(k):
        return jnp.minimum(k, last)

    def prev(k):
        return jnp.maximum(k - 1, 0)

    tok_prev = pl.BlockSpec((tm, d), lambda k: (prev(k), 0))
    return pl.pallas_call(
        functools.partial(_mixb_body, tm=tm, d=d),
        grid=(n // tm + 1,),
        in_specs=[
            pl.BlockSpec((None, tm, N_BRANCH * d), lambda k: (0, cur(k), 0)),
            pl.BlockSpec((None, d, m), lambda k: (cur(k) // nt, 0, 0)),
            pl.BlockSpec((None, m, d), lambda k: (cur(k) // nt, 0, 0)),
            pl.BlockSpec((None, tm, N_BRANCH * d), lambda k: (1, prev(k), 0)),
            tok_prev,
            tok_prev,
            _resident((None, 1, d), lambda k: (l, 0, 0)),
            _resident((None, 1, d), lambda k: (l, 0, 0)),
            _resident((None, SGU_GROUPS, SGU_CHUNK, SGU_CHUNK), lambda k: (l, 0, 0, 0)),
            _resident((None, SGU_GROUPS, SGU_CHUNK, 1), lambda k: (l, 0, 0, 0)),
            _resident((None, N_BRANCH, d, d), lambda k: (l, 0, 0, 0)),
            _resident((None, d, d), lambda k: (l, 0, 0)),
            _resident((None, 1, d), lambda k: (l * N_NORMS + 3, 0, 0)),
        ],
        out_specs=tok_prev,
        out_shape=jax.ShapeDtypeStruct((n, d), F32),
        scratch_shapes=[pltpu.VMEM((tm, d), BF16) for _ in range(4)] + [pltpu.VMEM((tm, d), F32) for _ in range(2)],
        compiler_params=_params(("arbitrary",)),
        name="mixb",
    )(z, kt, v, z, y_m, h, ln_g, ln_b, w_s, b_s, w_branch, w_out, norms)


def kernel(x, mem, norm_g, w_ffn_gu, w_ffn_down, w_in, conv_w, conv_b, b_i, b_f, mlstm_norm_g,
           sgu_ln_g, sgu_ln_b, sgu_w_s, sgu_b_s, w_mkv, w_branch, w_out):
    b, s, d = x.shape
    depth = norm_g.shape[0]
    dff = w_ffn_down.shape[2]
    n = b * s
    assert s % MIX_TM == 0 or s < MIX_TM
    assert dff % FFN_CW == 0

    norms = norm_g.reshape(depth * N_NORMS, 1, d)
    wgu = w_ffn_gu.astype(BF16)
    wd = w_ffn_down.astype(BF16)
    n_gate = 2 * NH_M
    c_if = N_Z_MLSTM * d
    w_in_bf = w_in.astype(BF16)
    w_mlstm = w_in_bf[:, :, :c_if]
    w_rest = w_in_bf[:, :, c_if + n_gate:]
    w_if = w_in_bf[:, :, c_if:c_if + GATE_LANES]
    b_if = jnp.pad(jnp.concatenate([b_i, b_f], axis=-1), ((0, 0), (0, GATE_LANES - n_gate)))
    b_if = b_if.reshape(depth, 1, GATE_LANES)
    wkv = w_mkv.astype(BF16)
    wb = w_branch.astype(BF16)
    wo = w_out.astype(BF16)
    conv_b3 = conv_b.reshape(depth, 1, d)
    ng3 = mlstm_norm_g.reshape(depth, 1, d)
    lng3 = sgu_ln_g.reshape(depth, 1, d)
    lnb3 = sgu_ln_b.reshape(depth, 1, d)
    bs4 = sgu_b_s.reshape(depth, SGU_GROUPS, SGU_CHUNK, 1)

    h = x.reshape(n, d)
    for l in range(depth):
        h = _ffn(h, norms, l, 0, 1, wgu, wd, 0)
        kt, v = _memkv(mem, norms, l, wkv)
        y_m, z = _mixa(h, b, s, norms, l, w_mlstm, w_rest, w_if, conv_w, conv_b3, b_if, ng3)
        h = _mixb(h, z, y_m, kt, v, s, lng3, lnb3, sgu_w_s, bs4, wb, wo, norms, l)
        h = _ffn(h, norms, l, 4, 5, wgu, wd, 1)
    return h.reshape(b, s, d)
```

```python
import functools

import jax
import jax.numpy as jnp
from jax import lax
from jax.experimental import pallas as pl
from jax.experimental.pallas import tpu as pltpu

F32 = jnp.float32
BF16 = jnp.bfloat16

EPS = 1e-6
NEG_INIT = -1e30
NH_M = 4
NH_X = 4
SGU_GROUPS = 4
SGU_CHUNK = 128
CONV_K = 4
N_BRANCH = 3
N_NORMS = 7
N_Z_MLSTM = 3
N_Z_REST = 6
GATE_LANES = 128
HIST = 8

VMEM_LIMIT_V7X = 56 * 1024 * 1024

FFN_TM = 1024
FFN_RS = 256
FFN_CW = 256
MIX_TM = 512
MLSTM_L = 256


def _params(semantics):
    return pltpu.CompilerParams(dimension_semantics=semantics, vmem_limit_bytes=VMEM_LIMIT_V7X)


def _resident(block_shape, index_map):
    return pl.BlockSpec(block_shape, index_map, pipeline_mode=pl.Buffered(1))


def _rms(x, g):
    return x * lax.rsqrt(jnp.mean(x * x, axis=-1, keepdims=True) + EPS) * g


def _dot(a, b):
    return jnp.dot(a, b, preferred_element_type=F32)


LOG2E = 1.4426950408889634
GELU_K = 2.0 * 0.7978845608028654
GELU_C = 0.044715


def _sigmoid(x):
    return 1.0 / (1.0 + jnp.exp2(x * (-LOG2E)))


def _gelu_tanh(x):
    t = x * ((-GELU_K * LOG2E) + (-GELU_K * GELU_C * LOG2E) * (x * x))
    return x / (1.0 + jnp.exp2(t))


def _ffn_body(x_ref, gpre_ref, gpost_ref, wgu_ref, wd_ref, o_ref, xn_ref, hid_ref, *, dff, cw, rs):
    n_sub = x_ref.shape[0] // rs
    nch = dff // cw

    def rows(r):
        return slice(r * rs, (r + 1) * rs)

    def normalise(r):
        xn_ref[rows(r), :] = _rms(x_ref[rows(r), :], gpre_ref[...]).astype(BF16)

    def hidden_chunk(r, c):
        xn = xn_ref[rows(r), :]
        a = _dot(xn, wgu_ref[:, c * cw:(c + 1) * cw])
        b = _dot(xn, wgu_ref[:, dff + c * cw:dff + (c + 1) * cw])
        hid_ref[rows(r), c * cw:(c + 1) * cw] = (a * jax.nn.sigmoid(a) * b).astype(BF16)

    def finish(r):
        f = _dot(hid_ref[rows(r), :], wd_ref[...])
        o_ref[rows(r), :] = x_ref[rows(r), :] + 0.5 * _rms(f, gpost_ref[...])

    normalise(0)
    for r in range(n_sub):
        for c in range(nch):
            hidden_chunk(r, c)
            if c == 1 and r + 1 < n_sub:
                normalise(r + 1)
            if c == 1 and r > 0:
                finish(r - 1)
    finish(n_sub - 1)


def _ffn(h, norms, l, k_pre, k_post, wgu, wd, ffn_idx):
    n, d = h.shape
    dff = wd.shape[2]
    tm = min(FFN_TM, n)
    return pl.pallas_call(
        functools.partial(_ffn_body, dff=dff, cw=FFN_CW, rs=min(FFN_RS, tm)),
        grid=(n // tm,),
        in_specs=[
            pl.BlockSpec((tm, d), lambda i: (i, 0)),
            _resident((None, 1, d), lambda i: (l * N_NORMS + k_pre, 0, 0)),
            _resident((None, 1, d), lambda i: (l * N_NORMS + k_post, 0, 0)),
            _resident((None, None, d, 2 * dff), lambda i: (l, ffn_idx, 0, 0)),
            _resident((None, None, dff, d), lambda i: (l, ffn_idx, 0, 0)),
        ],
        out_specs=pl.BlockSpec((tm, d), lambda i: (i, 0)),
        out_shape=jax.ShapeDtypeStruct((n, d), F32),
        scratch_shapes=[pltpu.VMEM((tm, d), BF16), pltpu.VMEM((tm, dff), BF16)],
        compiler_params=_params(("parallel",)),
        name="ffn",
    )(h, norms, norms, wgu, wd)


def _memkv_body(mem_ref, g_ref, wkv_ref, kt_ref, v_ref, *, d, scale):
    mn = _rms(mem_ref[...], g_ref[...]).astype(BF16)
    k = _dot(mn, wkv_ref[:, :d]) * scale
    kt_ref[...] = k.T.astype(BF16)
    v_ref[...] = _dot(mn, wkv_ref[:, d:]).astype(BF16)


def _memkv(mem, norms, l, wkv):
    b, m, d = mem.shape
    scale = float((d // NH_X) ** -0.5)
    return pl.pallas_call(
        functools.partial(_memkv_body, d=d, scale=scale),
        grid=(b,),
        in_specs=[
            pl.BlockSpec((None, m, d), lambda i: (i, 0, 0)),
            _resident((None, 1, d), lambda i: (l * N_NORMS + 6, 0, 0)),
            _resident((None, d, 2 * d), lambda i: (l, 0, 0)),
        ],
        out_specs=[
            pl.BlockSpec((None, d, m), lambda i: (i, 0, 0)),
            pl.BlockSpec((None, m, d), lambda i: (i, 0, 0)),
        ],
        out_shape=[
            jax.ShapeDtypeStruct((b, d, m), BF16),
            jax.ShapeDtypeStruct((b, m, d), BF16),
        ],
        compiler_params=_params(("parallel",)),
        name="memkv",
    )(mem, norms, wkv)


def _log_sigmoid(x):
    return -(jnp.maximum(-x, 0.0) + jnp.log1p(jnp.exp(-jnp.abs(x))))


def _split3(x):
    hi = x.astype(BF16)
    r1 = x - hi.astype(F32)
    mid = r1.astype(BF16)
    lo = (r1 - mid.astype(F32)).astype(BF16)
    return hi, mid, lo


def _mixa_body(h_ref, g_ref, wm_ref, wr_ref, wif_ref, cw_ref, cb_ref, bif_ref, ng_ref,
               y_ref, z_ref,
               xn_ref, xs_ref, q_ref, k_ref, v_ref, o_ref, gc_ref, gr_ref, c_ref, n_ref, m_ref,
               *, tm, chunk, d, dqk, dv):
    i = pl.program_id(1)
    nqk = NH_M * dqk
    n_chunks = tm // chunk
    zc = wr_ref.shape[-1] // n_chunks
    zw = zc // NH_M

    @pl.when(i == 0)
    def _():
        c_ref[...] = jnp.zeros_like(c_ref)
        n_ref[...] = jnp.zeros_like(n_ref)
        m_ref[...] = jnp.full_like(m_ref, NEG_INIT)
        xs_ref[0:HIST, :] = jnp.zeros((HIST, d), F32)

    @pl.when(i > 0)
    def _():
        xs_ref[0:HIST, :] = xs_ref[tm:tm + HIST, :]

    xn_ref[...] = _rms(h_ref[...], g_ref[...]).astype(BF16)
    xs_ref[HIST:, :] = _dot(xn_ref[...], wm_ref[:, 0:d])
    gp = _dot(xn_ref[...], wif_ref[...]) + bif_ref[...]
    v_ref[...] = _dot(xn_ref[...], wm_ref[:, d:2 * d]).astype(BF16)
    o_ref[...] = _dot(xn_ref[...], wm_ref[:, 2 * d:3 * d]).astype(BF16)

    lane = lax.broadcasted_iota(jnp.int32, gp.shape, 1)
    gates = jnp.where(lane < NH_M, gp, _log_sigmoid(gp))
    gc_ref[...] = gates
    gates_t = gates.T
    for c in range(n_chunks):
        gr_ref[c] = gates_t[0:2 * NH_M, c * chunk:(c + 1) * chunk]

    r_i = lax.broadcasted_iota(jnp.int32, (chunk, chunk), 0)
    c_i = lax.broadcasted_iota(jnp.int32, (chunk, chunk), 1)
    causal = r_i >= c_i

    def z_piece(c, p):
        cols = slice(p * zw, (p + 1) * zw)
        z_ref[c, :, cols] = _dot(xn_ref[...], wr_ref[:, c * zc + p * zw:c * zc + (p + 1) * zw]).astype(BF16)

    def chunk_body(c):
        rows = slice(c * chunk, (c + 1) * chunk)

        conv = cb_ref[...]
        for j in range(CONV_K):
            conv = conv + cw_ref[j:j + 1, :] * xs_ref[pl.ds(c * chunk + HIST - (CONV_K - 1) + j, chunk), :]
        qk = conv * _sigmoid(conv)
        q_ref[rows, :] = qk[:, :nqk]
        k_ref[rows, :] = qk[:, nqk:] * float(dqk ** -0.5)

        g_c = gc_ref[rows, :]
        g_r = gr_ref[c]
        tril = jnp.where(causal, 1.0, 0.0).astype(BF16)
        triu = jnp.where(r_i <= c_i, 1.0, 0.0).astype(BF16)
        cum_c = sum(_dot(tril, p) for p in _split3(g_c))
        cum_r = sum(_dot(p, triu) for p in _split3(g_r))
        for h in range(NH_M):
            z_piece(c, h)

            qc = q_ref[rows, h * dqk:(h + 1) * dqk]
            kc = k_ref[rows, h * dqk:(h + 1) * dqk]
            vc = v_ref[rows, h * dv:(h + 1) * dv]
            qb = qc.astype(BF16)
            i_col = g_c[:, h:h + 1]
            b_col = cum_c[:, NH_M + h:NH_M + h + 1]
            i_row = g_r[h:h + 1, :]
            b_row = cum_r[NH_M + h:NH_M + h + 1, :]
            m_prev = m_ref[h, :, 0:1]
            g_tot = b_col[chunk - 1:chunk, :]

            log_d = jnp.where(causal, b_col + (i_row - b_row), -jnp.inf)
            m_inter = b_col + m_prev
            m_t = jnp.maximum(m_inter, jnp.max(log_d, axis=-1, keepdims=True))
            d_mat = jnp.exp(log_d - m_t)
            s = lax.dot_general(qb, kc.astype(BF16), (((1,), (1,)), ((), ())),
                                preferred_element_type=F32) * d_mat
            inter = jnp.exp(m_inter - m_t)
            num = _dot(s.astype(BF16), vc) + inter * _dot(qb, c_ref[h].astype(BF16))
            den = (jnp.sum(s, axis=-1, keepdims=True)
                   + inter * jnp.sum(qc * n_ref[h], axis=-1, keepdims=True))
            hh = num / jnp.maximum(jnp.abs(den), jnp.exp(-m_t))

            a_col = g_tot - b_col + i_col
            m_new = jnp.maximum(g_tot + m_prev, jnp.max(a_col, axis=0, keepdims=True))
            wk = jnp.exp(a_col - m_new) * kc
            decay = jnp.exp(g_tot + m_prev - m_new)
            c_ref[h] = decay * c_ref[h] + lax.dot_general(
                wk.astype(BF16), vc, (((0,), (0,)), ((), ())), preferred_element_type=F32)
            n_ref[h] = decay * n_ref[h] + jnp.sum(wk, axis=0, keepdims=True)
            m_ref[h] = jnp.broadcast_to(m_new, m_ref.shape[1:])

            mu = jnp.mean(hh, axis=-1, keepdims=True)
            hc = hh - mu
            hn = hc * lax.rsqrt(jnp.mean(hc * hc, axis=-1, keepdims=True) + EPS)
            hn = hn * ng_ref[:, h * dv:(h + 1) * dv]
            og = _sigmoid(o_ref[rows, h * dv:(h + 1) * dv].astype(F32))
            y_ref[rows, h * dv:(h + 1) * dv] = (hn * og).astype(BF16)

    for c in range(n_chunks):
        pl.when(i >= -c)(functools.partial(chunk_body, c))


def _mixa(h, b, s, norms, l, w_mlstm, w_rest, w_if, conv_w, conv_b, b_if, norm_g):
    n, d = h.shape
    dv = d // NH_M
    dqk = dv // 2
    tm = min(MIX_TM, s)
    chunk = min(MLSTM_L, tm)
    nt = s // tm
    n_chunks = tm // chunk
    zc = w_rest.shape[2] // n_chunks
    assert zc * n_chunks == w_rest.shape[2] and zc % (NH_M * 128) == 0
    return pl.pallas_call(
        functools.partial(_mixa_body, tm=tm, chunk=chunk, d=d, dqk=dqk, dv=dv),
        grid=(b, nt),
        in_specs=[
            pl.BlockSpec((tm, d), lambda bi, i: (bi * nt + i, 0)),
            _resident((None, 1, d), lambda bi, i: (l * N_NORMS + 2, 0, 0)),
            _resident((None, d, N_Z_MLSTM * d), lambda bi, i: (l, 0, 0)),
            _resident((None, d, n_chunks * zc), lambda bi, i: (l, 0, 0)),
            _resident((None, d, GATE_LANES), lambda bi, i: (l, 0, 0)),
            _resident((None, CONV_K, d), lambda bi, i: (l, 0, 0)),
            _resident((None, 1, d), lambda bi, i: (l, 0, 0)),
            _resident((None, 1, GATE_LANES), lambda bi, i: (l, 0, 0)),
            _resident((None, 1, d), lambda bi, i: (l, 0, 0)),
        ],
        out_specs=[
            pl.BlockSpec((tm, d), lambda bi, i: (bi * nt + i, 0)),
            pl.BlockSpec((n_chunks, tm, zc), lambda bi, i: (0, bi * nt + i, 0)),
        ],
        out_shape=[
            jax.ShapeDtypeStruct((n, d), BF16),
            jax.ShapeDtypeStruct((n_chunks, n, zc), BF16),
        ],
        scratch_shapes=[
            pltpu.VMEM((tm, d), BF16),
            pltpu.VMEM((tm + HIST, d), F32),
            pltpu.VMEM((tm, NH_M * dqk), F32),
            pltpu.VMEM((tm, NH_M * dqk), F32),
            pltpu.VMEM((tm, d), BF16),
            pltpu.VMEM((tm, d), BF16),
            pltpu.VMEM((tm, GATE_LANES), F32),
            pltpu.VMEM((n_chunks, 2 * NH_M, chunk), F32),
            pltpu.VMEM((NH_M, dqk, dv), F32),
            pltpu.VMEM((NH_M, 1, dqk), F32),
            pltpu.VMEM((NH_M, 1, GATE_LANES), F32),
        ],
        compiler_params=_params(("arbitrary", "arbitrary")),
        name="mixa",
    )(h, norms, w_mlstm, w_rest, w_if, conv_w, conv_b, b_if, norm_g)


def _mixb_body(zf_ref, kt_ref, v_ref, zg_ref, ym_ref, h_ref, lng_ref, lnb_ref, ws_ref, bs_ref, wb_ref, wo_ref,
               g_ref, o_ref, ys_ref, yx_ref, ysp_ref, yxp_ref, mg_ref, yo_ref, *, tm, d):
    k = pl.program_id(0)
    dg = d // SGU_GROUPS
    dh = d // NH_X
    zu, zv, zq = 0, d, 2 * d

    @pl.when(k == 0)
    def _():
        ys_ref[...] = jnp.zeros((tm, d), BF16)
        yx_ref[...] = jnp.zeros((tm, d), BF16)

    ysp_ref[...] = ys_ref[...]
    yxp_ref[...] = yx_ref[...]

    r_i = lax.broadcasted_iota(jnp.int32, (SGU_CHUNK, SGU_CHUNK), 0)
    c_i = lax.broadcasted_iota(jnp.int32, (SGU_CHUNK, SGU_CHUNK), 1)
    causal = r_i >= c_i

    def sgu_group(g):
        cols = slice(g * dg, (g + 1) * dg)
        v = _gelu_tanh(zf_ref[:, zv + g * dg:zv + (g + 1) * dg].astype(F32))
        mu = jnp.mean(v, axis=-1, keepdims=True)
        vc = v - mu
        vn = vc * lax.rsqrt(jnp.mean(vc * vc, axis=-1, keepdims=True) + EPS) * lng_ref[:, cols]
        vn = (vn + lnb_ref[:, cols]).astype(BF16)
        w = jnp.where(causal, ws_ref[g], 0.0).astype(BF16)
        for c in range(tm // SGU_CHUNK):
            rows = slice(c * SGU_CHUNK, (c + 1) * SGU_CHUNK)
            mixed = _dot(w, vn[rows, :]) + bs_ref[g]
            u = _gelu_tanh(zf_ref[rows, zu + g * dg:zu + (g + 1) * dg].astype(F32))
            ys_ref[rows, cols] = (u * mixed).astype(BF16)

    def xattn_head(h):
        cols = slice(h * dh, (h + 1) * dh)
        s = _dot(zf_ref[:, zq + h * dh:zq + (h + 1) * dh], kt_ref[cols, :])
        p = jnp.exp(s - jnp.max(s, axis=-1, keepdims=True))
        p = p / jnp.sum(p, axis=-1, keepdims=True)
        yx_ref[:, cols] = _dot(p.astype(BF16), v_ref[:, cols]).astype(BF16)

    half = d // 2

    def branch_half(j, y_ref, hf):
        cols = slice(hf * half, (hf + 1) * half)
        gate = _sigmoid(zg_ref[:, j * d + hf * half:j * d + (hf + 1) * half].astype(F32))
        term = gate * _dot(y_ref[...], wb_ref[j, :, cols])
        if j == 0:
            mg_ref[:, cols] = term
        else:
            mg_ref[:, cols] += term

    def out_half(hf):
        cols = slice(hf * half, (hf + 1) * half)
        yo_ref[:, cols] = _dot(mg_ref[...].astype(BF16), wo_ref[:, cols])

    sgu_group(0)
    branch_half(0, ym_ref, 0)
    sgu_group(1)
    branch_half(0, ym_ref, 1)
    sgu_group(2)
    branch_half(1, ysp_ref, 0)
    sgu_group(3)
    branch_half(1, ysp_ref, 1)
    xattn_head(0)
    branch_half(2, yxp_ref, 0)
    xattn_head(1)
    branch_half(2, yxp_ref, 1)
    xattn_head(2)
    out_half(0)
    xattn_head(3)
    out_half(1)
    o_ref[...] = h_ref[...] + _rms(yo_ref[...], g_ref[...])


def _mixb(h, z, y_m, kt, v, s, ln_g, ln_b, w_s, b_s, w_branch, w_out, norms, l):
    n, d = h.shape
    m = v.shape[1]
    tm = min(MIX_TM, s)
    nt = s // tm
    last = n // tm - 1
    assert z.shape[0] == 2 and z.shape[2] == N_BRANCH * d

    def cur(k):
        return jnp.minimum(k, last)

    def prev(k):
        return jnp.maximum(k - 1, 0)

    tok_prev = pl.BlockSpec((tm, d), lambda k: (prev(k), 0))
    return pl.pallas_call(
        functools.partial(_mixb_body, tm=tm, d=d),
        grid=(n // tm + 1,),
        in_specs=[
            pl.BlockSpec((None, tm, N_BRANCH * d), lambda k: (0, cur(k), 0)),
            pl.BlockSpec((None, d, m), lambda k: (cur(k) // nt, 0, 0)),
            pl.BlockSpec((None, m, d), lambda k: (cur(k) // nt, 0, 0)),
            pl.BlockSpec((None, tm, N_BRANCH * d), lambda k: (1, prev(k), 0)),
            tok_prev,
            tok_prev,
            _resident((None, 1, d), lambda k: (l, 0, 0)),
            _resident((None, 1, d), lambda k: (l, 0, 0)),
            _resident((None, SGU_GROUPS, SGU_CHUNK, SGU_CHUNK), lambda k: (l, 0, 0, 0)),
            _resident((None, SGU_GROUPS, SGU_CHUNK, 1), lambda k: (l, 0, 0, 0)),
            _resident((None, N_BRANCH, d, d), lambda k: (l, 0, 0, 0)),
            _resident((None, d, d), lambda k: (l, 0, 0)),
            _resident((None, 1, d), lambda k: (l * N_NORMS + 3, 0, 0)),
        ],
        out_specs=tok_prev,
        out_shape=jax.ShapeDtypeStruct((n, d), F32),
        scratch_shapes=[pltpu.VMEM((tm, d), BF16) for _ in range(4)] + [pltpu.VMEM((tm, d), F32) for _ in range(2)],
        compiler_params=_params(("arbitrary",)),
        name="mixb",
    )(z, kt, v, z, y_m, h, ln_g, ln_b, w_s, b_s, w_branch, w_out, norms)


def kernel(x, mem, norm_g, w_ffn_gu, w_ffn_down, w_in, conv_w, conv_b, b_i, b_f, mlstm_norm_g,
           sgu_ln_g, sgu_ln_b, sgu_w_s, sgu_b_s, w_mkv, w_branch, w_out):
    b, s, d = x.shape
    depth = norm_g.shape[0]
    dff = w_ffn_down.shape[2]
    n = b * s
    assert s % MIX_TM == 0 or s < MIX_TM
    assert dff % FFN_CW == 0

    norms = norm_g.reshape(depth * N_NORMS, 1, d)
    wgu = w_ffn_gu.astype(BF16)
    wd = w_ffn_down.astype(BF16)
    n_gate = 2 * NH_M
    c_if = N_Z_MLSTM * d
    w_mlstm = w_in[:, :, :c_if].astype(BF16)
    w_rest = w_in[:, :, c_if + n_gate:].astype(BF16)
    w_if = w_in[:, :, c_if:c_if + GATE_LANES].astype(BF16)
    b_if = jnp.pad(jnp.concatenate([b_i, b_f], axis=-1), ((0, 0), (0, GATE_LANES - n_gate)))
    b_if = b_if.reshape(depth, 1, GATE_LANES)
    wkv = w_mkv.astype(BF16)
    wb = w_branch.astype(BF16)
    wo = w_out.astype(BF16)
    conv_b3 = conv_b.reshape(depth, 1, d)
    ng3 = mlstm_norm_g.reshape(depth, 1, d)
    lng3 = sgu_ln_g.reshape(depth, 1, d)
    lnb3 = sgu_ln_b.reshape(depth, 1, d)
    bs4 = sgu_b_s.reshape(depth, SGU_GROUPS, SGU_CHUNK, 1)

    h = x.reshape(n, d)
    for l in range(depth):
        h = _ffn(h, norms, l, 0, 1, wgu, wd, 0)
        kt, v = _memkv(mem, norms, l, wkv)
        y_m, z = _mixa(h, b, s, norms, l, w_mlstm, w_rest, w_if, conv_w, conv_b3, b_if, ng3)
        h = _mixb(h, z, y_m, kt, v, s, lng3, lnb3, sgu_w_s, bs4, wb, wo, norms, l)
        h = _ffn(h, norms, l, 4, 5, wgu, wd, 1)
    return h.reshape(b, s, d)
```

```python
import functools

import jax
import jax.numpy as jnp
from jax import lax
from jax.experimental import pallas as pl
from jax.experimental.pallas import tpu as pltpu

F32 = jnp.float32
BF16 = jnp.bfloat16

EPS = 1e-6
NEG_INIT = -1e30
NH_M = 4
NH_X = 4
SGU_GROUPS = 4
SGU_CHUNK = 128
CONV_K = 4
N_BRANCH = 3
N_NORMS = 7
N_Z_MLSTM = 3
N_Z_REST = 6
GATE_LANES = 128
HIST = 8

VMEM_LIMIT_V7X = 56 * 1024 * 1024

FFN_TM = 1024
FFN_RS = 512
FFN_CW = 256
MIX_TM = 512
MLSTM_L = 256
PREP_ROWS = 256


def _params(semantics):
    return pltpu.CompilerParams(dimension_semantics=semantics, vmem_limit_bytes=VMEM_LIMIT_V7X)


def _resident(block_shape, index_map):
    return pl.BlockSpec(block_shape, index_map, pipeline_mode=pl.Buffered(1))


def _rms(x, g):
    return x * lax.rsqrt(jnp.mean(x * x, axis=-1, keepdims=True) + EPS) * g


def _dot(a, b):
    return jnp.dot(a, b, preferred_element_type=F32)


LOG2E = 1.4426950408889634
GELU_K = 2.0 * 0.7978845608028654
GELU_C = 0.044715


def _sigmoid(x):
    return 1.0 / (1.0 + jnp.exp2(x * (-LOG2E)))


def _gelu_tanh(x):
    t = x * ((-GELU_K * LOG2E) + (-GELU_K * GELU_C * LOG2E) * (x * x))
    return x / (1.0 + jnp.exp2(t))


def _ffn_body(x_ref, gpre_ref, gpost_ref, wgu_ref, wd_ref, o_ref, xn_ref, hid_ref, *, dff, cw, rs):
    n_sub = x_ref.shape[0] // rs
    nch = dff // cw

    def rows(r):
        return slice(r * rs, (r + 1) * rs)

    def normalise(r):
        xn_ref[rows(r), :] = _rms(x_ref[rows(r), :], gpre_ref[...]).astype(BF16)

    def hidden_chunk(r, c):
        xn = xn_ref[rows(r), :]
        a = _dot(xn, wgu_ref[:, c * cw:(c + 1) * cw])
        b = _dot(xn, wgu_ref[:, dff + c * cw:dff + (c + 1) * cw])
        hid_ref[rows(r), c * cw:(c + 1) * cw] = (a * jax.nn.sigmoid(a) * b).astype(BF16)

    def finish(r):
        f = _dot(hid_ref[rows(r), :], wd_ref[...])
        o_ref[rows(r), :] = x_ref[rows(r), :] + 0.5 * _rms(f, gpost_ref[...])

    normalise(0)
    for r in range(n_sub):
        for c in range(nch):
            hidden_chunk(r, c)
            if c == 1 and r + 1 < n_sub:
                normalise(r + 1)
            if c == 1 and r > 0:
                finish(r - 1)
    finish(n_sub - 1)


def _ffn(h, norms, l, k_pre, k_post, wgu, wd, ffn_idx):
    n, d = h.shape
    dff = wd.shape[2]
    tm = min(FFN_TM, n)
    return pl.pallas_call(
        functools.partial(_ffn_body, dff=dff, cw=FFN_CW, rs=min(FFN_RS, tm)),
        grid=(n // tm,),
        in_specs=[
            pl.BlockSpec((tm, d), lambda i: (i, 0)),
            _resident((None, 1, d), lambda i: (l * N_NORMS + k_pre, 0, 0)),
            _resident((None, 1, d), lambda i: (l * N_NORMS + k_post, 0, 0)),
            _resident((None, None, d, 2 * dff), lambda i: (l, ffn_idx, 0, 0)),
            _resident((None, None, dff, d), lambda i: (l, ffn_idx, 0, 0)),
        ],
        out_specs=pl.BlockSpec((tm, d), lambda i: (i, 0)),
        out_shape=jax.ShapeDtypeStruct((n, d), F32),
        scratch_shapes=[pltpu.VMEM((tm, d), BF16), pltpu.VMEM((tm, dff), BF16)],
        compiler_params=_params(("parallel",)),
        name="ffn",
    )(h, norms, norms, wgu, wd)


def _prep_win_body(w_ref, wm_ref, wif_ref, wr_ref, *, c_if, n_gate):
    wm_ref[...] = w_ref[:, :c_if].astype(BF16)
    wif_ref[...] = w_ref[:, c_if:c_if + GATE_LANES].astype(BF16)
    wr_ref[...] = w_ref[:, c_if + n_gate:].astype(BF16)


def _prep_win(w_in, d):
    depth, _, cols = w_in.shape
    c_if = N_Z_MLSTM * d
    n_gate = 2 * NH_M
    c_rest = cols - c_if - n_gate
    rb = min(PREP_ROWS, d)
    return pl.pallas_call(
        functools.partial(_prep_win_body, c_if=c_if, n_gate=n_gate),
        grid=(depth, d // rb),
        in_specs=[pl.BlockSpec((None, rb, cols), lambda l, r: (l, r, 0))],
        out_specs=[
            pl.BlockSpec((None, rb, c_if), lambda l, r: (l, r, 0)),
            pl.BlockSpec((None, rb, GATE_LANES), lambda l, r: (l, r, 0)),
            pl.BlockSpec((None, rb, c_rest), lambda l, r: (l, r, 0)),
        ],
        out_shape=[
            jax.ShapeDtypeStruct((depth, d, c_if), BF16),
            jax.ShapeDtypeStruct((depth, d, GATE_LANES), BF16),
            jax.ShapeDtypeStruct((depth, d, c_rest), BF16),
        ],
        compiler_params=_params(("parallel", "parallel")),
        name="prep_win",
    )(w_in)


def _memkv_body(mem_ref, g_ref, wkv_ref, kt_ref, v_ref, *, d, scale):
    mn = _rms(mem_ref[...], g_ref[...]).astype(BF16)
    k = _dot(mn, wkv_ref[:, :d]) * scale
    kt_ref[...] = k.T.astype(BF16)
    v_ref[...] = _dot(mn, wkv_ref[:, d:]).astype(BF16)


def _memkv(mem, norms, l, wkv):
    b, m, d = mem.shape
    scale = float((d // NH_X) ** -0.5)
    return pl.pallas_call(
        functools.partial(_memkv_body, d=d, scale=scale),
        grid=(b,),
        in_specs=[
            pl.BlockSpec((None, m, d), lambda i: (i, 0, 0)),
            _resident((None, 1, d), lambda i: (l * N_NORMS + 6, 0, 0)),
            _resident((None, d, 2 * d), lambda i: (l, 0, 0)),
        ],
        out_specs=[
            pl.BlockSpec((None, d, m), lambda i: (i, 0, 0)),
            pl.BlockSpec((None, m, d), lambda i: (i, 0, 0)),
        ],
        out_shape=[
            jax.ShapeDtypeStruct((b, d, m), BF16),
            jax.ShapeDtypeStruct((b, m, d), BF16),
        ],
        compiler_params=_params(("parallel",)),
        name="memkv",
    )(mem, norms, wkv)


def _log_sigmoid(x):
    return -(jnp.maximum(-x, 0.0) + jnp.log1p(jnp.exp(-jnp.abs(x))))


def _split3(x):
    hi = x.astype(BF16)
    r1 = x - hi.astype(F32)
    mid = r1.astype(BF16)
    lo = (r1 - mid.astype(F32)).astype(BF16)
    return hi, mid, lo


def _mixa_body(h_ref, g_ref, wm_ref, wr_ref, wif_ref, cw_ref, cb_ref, bif_ref, ng_ref,
               y_ref, z_ref,
               xn_ref, xs_ref, q_ref, k_ref, v_ref, o_ref, gc_ref, gr_ref, c_ref, n_ref, m_ref,
               *, tm, chunk, d, dqk, dv):
    i = pl.program_id(1)
    nqk = NH_M * dqk
    n_chunks = tm // chunk
    zc = wr_ref.shape[-1] // n_chunks
    zw = zc // NH_M

    @pl.when(i == 0)
    def _():
        c_ref[...] = jnp.zeros_like(c_ref)
        n_ref[...] = jnp.zeros_like(n_ref)
        m_ref[...] = jnp.full_like(m_ref, NEG_INIT)
        xs_ref[0:HIST, :] = jnp.zeros((HIST, d), F32)

    @pl.when(i > 0)
    def _():
        xs_ref[0:HIST, :] = xs_ref[tm:tm + HIST, :]

    xn_ref[...] = _rms(h_ref[...], g_ref[...]).astype(BF16)
    xs_ref[HIST:, :] = _dot(xn_ref[...], wm_ref[:, 0:d])
    gp = _dot(xn_ref[...], wif_ref[...]) + bif_ref[...]
    v_ref[...] = _dot(xn_ref[...], wm_ref[:, d:2 * d]).astype(BF16)
    o_ref[...] = _dot(xn_ref[...], wm_ref[:, 2 * d:3 * d]).astype(BF16)

    lane = lax.broadcasted_iota(jnp.int32, gp.shape, 1)
    gates = jnp.where(lane < NH_M, gp, _log_sigmoid(gp))
    gc_ref[...] = gates
    gates_t = gates.T
    for c in range(n_chunks):
        gr_ref[c] = gates_t[0:2 * NH_M, c * chunk:(c + 1) * chunk]

    r_i = lax.broadcasted_iota(jnp.int32, (chunk, chunk), 0)
    c_i = lax.broadcasted_iota(jnp.int32, (chunk, chunk), 1)
    causal = r_i >= c_i

    def z_piece(c, p):
        cols = slice(p * zw, (p + 1) * zw)
        z_ref[c, :, cols] = _dot(xn_ref[...], wr_ref[:, c * zc + p * zw:c * zc + (p + 1) * zw]).astype(BF16)

    def chunk_body(c):
        rows = slice(c * chunk, (c + 1) * chunk)

        conv = cb_ref[...]
        for j in range(CONV_K):
            conv = conv + cw_ref[j:j + 1, :] * xs_ref[pl.ds(c * chunk + HIST - (CONV_K - 1) + j, chunk), :]
        qk = conv * _sigmoid(conv)
        q_ref[rows, :] = qk[:, :nqk]
        k_ref[rows, :] = qk[:, nqk:] * float(dqk ** -0.5)

        g_c = gc_ref[rows, :]
        g_r = gr_ref[c]
        tril = jnp.where(causal, 1.0, 0.0).astype(BF16)
        triu = jnp.where(r_i <= c_i, 1.0, 0.0).astype(BF16)
        cum_c = sum(_dot(tril, p) for p in _split3(g_c))
        cum_r = sum(_dot(p, triu) for p in _split3(g_r))
        for h in range(NH_M):
            z_piece(c, h)

            qc = q_ref[rows, h * dqk:(h + 1) * dqk]
            kc = k_ref[rows, h * dqk:(h + 1) * dqk]
            vc = v_ref[rows, h * dv:(h + 1) * dv]
            qb = qc.astype(BF16)
            i_col = g_c[:, h:h + 1]
            b_col = cum_c[:, NH_M + h:NH_M + h + 1]
            i_row = g_r[h:h + 1, :]
            b_row = cum_r[NH_M + h:NH_M + h + 1, :]
            m_prev = m_ref[h, :, 0:1]
            g_tot = b_col[chunk - 1:chunk, :]

            log_d = jnp.where(causal, b_col + (i_row - b_row), -jnp.inf)
            m_inter = b_col + m_prev
            m_t = jnp.maximum(m_inter, jnp.max(log_d, axis=-1, keepdims=True))
            d_mat = jnp.exp(log_d - m_t)
            s = lax.dot_general(qb, kc.astype(BF16), (((1,), (1,)), ((), ())),
                                preferred_element_type=F32) * d_mat
            inter = jnp.exp(m_inter - m_t)
            num = _dot(s.astype(BF16), vc) + inter * _dot(qb, c_ref[h].astype(BF16))
            den = (jnp.sum(s, axis=-1, keepdims=True)
                   + inter * jnp.sum(qc * n_ref[h], axis=-1, keepdims=True))
            hh = num / jnp.maximum(jnp.abs(den), jnp.exp(-m_t))

            a_col = g_tot - b_col + i_col
            m_new = jnp.maximum(g_tot + m_prev, jnp.max(a_col, axis=0, keepdims=True))
            wk = jnp.exp(a_col - m_new) * kc
            decay = jnp.exp(g_tot + m_prev - m_new)
            c_ref[h] = decay * c_ref[h] + lax.dot_general(
                wk.astype(BF16), vc, (((0,), (0,)), ((), ())), preferred_element_type=F32)
            n_ref[h] = decay * n_ref[h] + jnp.sum(wk, axis=0, keepdims=True)
            m_ref[h] = jnp.broadcast_to(m_new, m_ref.shape[1:])

            mu = jnp.mean(hh, axis=-1, keepdims=True)
            hc = hh - mu
            hn = hc * lax.rsqrt(jnp.mean(hc * hc, axis=-1, keepdims=True) + EPS)
            hn = hn * ng_ref[:, h * dv:(h + 1) * dv]
            og = _sigmoid(o_ref[rows, h * dv:(h + 1) * dv].astype(F32))
            y_ref[rows, h * dv:(h + 1) * dv] = (hn * og).astype(BF16)

    for c in range(n_chunks):
        pl.when(i >= -c)(functools.partial(chunk_body, c))


def _mixa(h, b, s, norms, l, w_mlstm, w_rest, w_if, conv_w, conv_b, b_if, norm_g):
    n, d = h.shape
    dv = d // NH_M
    dqk = dv // 2
    tm = min(MIX_TM, s)
    chunk = min(MLSTM_L, tm)
    nt = s // tm
    n_chunks = tm // chunk
    zc = w_rest.shape[2] // n_chunks
    assert zc * n_chunks == w_rest.shape[2] and zc % (NH_M * 128) == 0
    return pl.pallas_call(
        functools.partial(_mixa_body, tm=tm, chunk=chunk, d=d, dqk=dqk, dv=dv),
        grid=(b, nt),
        in_specs=[
            pl.BlockSpec((tm, d), lambda bi, i: (bi * nt + i, 0)),
            _resident((None, 1, d), lambda bi, i: (l * N_NORMS + 2, 0, 0)),
            _resident((None, d, N_Z_MLSTM * d), lambda bi, i: (l, 0, 0)),
            _resident((None, d, n_chunks * zc), lambda bi, i: (l, 0, 0)),
            _resident((None, d, GATE_LANES), lambda bi, i: (l, 0, 0)),
            _resident((None, CONV_K, d), lambda bi, i: (l, 0, 0)),
            _resident((None, 1, d), lambda bi, i: (l, 0, 0)),
            _resident((None, 1, GATE_LANES), lambda bi, i: (l, 0, 0)),
            _resident((None, 1, d), lambda bi, i: (l, 0, 0)),
        ],
        out_specs=[
            pl.BlockSpec((tm, d), lambda bi, i: (bi * nt + i, 0)),
            pl.BlockSpec((n_chunks, tm, zc), lambda bi, i: (0, bi * nt + i, 0)),
        ],
        out_shape=[
            jax.ShapeDtypeStruct((n, d), BF16),
            jax.ShapeDtypeStruct((n_chunks, n, zc), BF16),
        ],
        scratch_shapes=[
            pltpu.VMEM((tm, d), BF16),
            pltpu.VMEM((tm + HIST, d), F32),
            pltpu.VMEM((tm, NH_M * dqk), F32),
            pltpu.VMEM((tm, NH_M * dqk), F32),
            pltpu.VMEM((tm, d), BF16),
            pltpu.VMEM((tm, d), BF16),
            pltpu.VMEM((tm, GATE_LANES), F32),
            pltpu.VMEM((n_chunks, 2 * NH_M, chunk), F32),
            pltpu.VMEM((NH_M, dqk, dv), F32),
            pltpu.VMEM((NH_M, 1, dqk), F32),
            pltpu.VMEM((NH_M, 1, GATE_LANES), F32),
        ],
        compiler_params=_params(("arbitrary", "arbitrary")),
        name="mixa",
    )(h, norms, w_mlstm, w_rest, w_if, conv_w, conv_b, b_if, norm_g)


def _mixb_body(zf_ref, kt_ref, v_ref, zg_ref, ym_ref, h_ref, lng_ref, lnb_ref, ws_ref, bs_ref, wb_ref, wo_ref,
               g_ref, o_ref, ys_ref, yx_ref, ysp_ref, yxp_ref, mg_ref, yo_ref, *, tm, d):
    k = pl.program_id(0)
    dg = d // SGU_GROUPS
    dh = d // NH_X
    zu, zv, zq = 0, d, 2 * d

    @pl.when(k == 0)
    def _():
        ys_ref[...] = jnp.zeros((tm, d), BF16)
        yx_ref[...] = jnp.zeros((tm, d), BF16)

    ysp_ref[...] = ys_ref[...]
    yxp_ref[...] = yx_ref[...]

    r_i = lax.broadcasted_iota(jnp.int32, (SGU_CHUNK, SGU_CHUNK), 0)
    c_i = lax.broadcasted_iota(jnp.int32, (SGU_CHUNK, SGU_CHUNK), 1)
    causal = r_i >= c_i

    def sgu_group(g):
        cols = slice(g * dg, (g + 1) * dg)
        v = _gelu_tanh(zf_ref[:, zv + g * dg:zv + (g + 1) * dg].astype(F32))
        mu = jnp.mean(v, axis=-1, keepdims=True)
        vc = v - mu
        vn = vc * lax.rsqrt(jnp.mean(vc * vc, axis=-1, keepdims=True) + EPS) * lng_ref[:, cols]
        vn = (vn + lnb_ref[:, cols]).astype(BF16)
        w = jnp.where(causal, ws_ref[g], 0.0).astype(BF16)
        for c in range(tm // SGU_CHUNK):
            rows = slice(c * SGU_CHUNK, (c + 1) * SGU_CHUNK)
            mixed = _dot(w, vn[rows, :]) + bs_ref[g]
            u = _gelu_tanh(zf_ref[rows, zu + g * dg:zu + (g + 1) * dg].astype(F32))
            ys_ref[rows, cols] = (u * mixed).astype(BF16)

    def xattn_head(h):
        cols = slice(h * dh, (h + 1) * dh)
        s = _dot(zf_ref[:, zq + h * dh:zq + (h + 1) * dh], kt_ref[cols, :])
        p = jnp.exp(s - jnp.max(s, axis=-1, keepdims=True))
        p = p / jnp.sum(p, axis=-1, keepdims=True)
        yx_ref[:, cols] = _dot(p.astype(BF16), v_ref[:, cols]).astype(BF16)

    half = d // 2

    def branch_half(j, y_ref, hf):
        cols = slice(hf * half, (hf + 1) * half)
        gate = _sigmoid(zg_ref[:, j * d + hf * half:j * d + (hf + 1) * half].astype(F32))
        term = gate * _dot(y_ref[...], wb_ref[j, :, cols])
        if j == 0:
            mg_ref[:, cols] = term
        else:
            mg_ref[:, cols] += term

    def out_half(hf):
        cols = slice(hf * half, (hf + 1) * half)
        yo_ref[:, cols] = _dot(mg_ref[...].astype(BF16), wo_ref[:, cols])

    sgu_group(0)
    branch_half(0, ym_ref, 0)
    sgu_group(1)
    branch_half(0, ym_ref, 1)
    sgu_group(2)
    branch_half(1, ysp_ref, 0)
    sgu_group(3)
    branch_half(1, ysp_ref, 1)
    xattn_head(0)
    branch_half(2, yxp_ref, 0)
    xattn_head(1)
    branch_half(2, yxp_ref, 1)
    xattn_head(2)
    out_half(0)
    xattn_head(3)
    out_half(1)
    o_ref[...] = h_ref[...] + _rms(yo_ref[...], g_ref[...])


def _mixb(h, z, y_m, kt, v, s, ln_g, ln_b, w_s, b_s, w_branch, w_out, norms, l):
    n, d = h.shape
    m = v.shape[1]
    tm = min(MIX_TM, s)
    nt = s // tm
    last = n // tm - 1
    assert z.shape[0] == 2 and z.shape[2] == N_BRANCH * d

    def cur(k):
        return jnp.minimum(k, last)

    def prev(k):
        return jnp.maximum(k - 1, 0)

    tok_prev = pl.BlockSpec((tm, d), lambda k: (prev(k), 0))
    return pl.pallas_call(
        functools.partial(_mixb_body, tm=tm, d=d),
        grid=(n // tm + 1,),
        in_specs=[
            pl.BlockSpec((None, tm, N_BRANCH * d), lambda k: (0, cur(k), 0)),
            pl.BlockSpec((None, d, m), lambda k: (cur(k) // nt, 0, 0)),
            pl.BlockSpec((None, m, d), lambda k: (cur(k) // nt, 0, 0)),
            pl.BlockSpec((None, tm, N_BRANCH * d), lambda k: (1, prev(k), 0)),
            tok_prev,
            tok_prev,
            _resident((None, 1, d), lambda k: (l, 0, 0)),
            _resident((None, 1, d), lambda k: (l, 0, 0)),
            _resident((None, SGU_GROUPS, SGU_CHUNK, SGU_CHUNK), lambda k: (l, 0, 0, 0)),
            _resident((None, SGU_GROUPS, SGU_CHUNK, 1), lambda k: (l, 0, 0, 0)),
            _resident((None, N_BRANCH, d, d), lambda k: (l, 0, 0, 0)),
            _resident((None, d, d), lambda k: (l, 0, 0)),
            _resident((None, 1, d), lambda k: (l * N_NORMS + 3, 0, 0)),
        ],
        out_specs=tok_prev,
        out_shape=jax.ShapeDtypeStruct((n, d), F32),
        scratch_shapes=[pltpu.VMEM((tm, d), BF16) for _ in range(4)] + [pltpu.VMEM((tm, d), F32) for _ in range(2)],
        compiler_params=_params(("arbitrary",)),
        name="mixb",
    )(z, kt, v, z, y_m, h, ln_g, ln_b, w_s, b_s, w_branch, w_out, norms)


def kernel(x, mem, norm_g, w_ffn_gu, w_ffn_down, w_in, conv_w, conv_b, b_i, b_f, mlstm_norm_g,
           sgu_ln_g, sgu_ln_b, sgu_w_s, sgu_b_s, w_mkv, w_branch, w_out):
    b, s, d = x.shape
    depth = norm_g.shape[0]
    dff = w_ffn_down.shape[2]
    n = b * s
    assert s % MIX_TM == 0 or s < MIX_TM
    assert dff % FFN_CW == 0

    norms = norm_g.reshape(depth * N_NORMS, 1, d)
    wgu = w_ffn_gu.astype(BF16)
    wd = w_ffn_down.astype(BF16)
    w_mlstm, w_if, w_rest = _prep_win(w_in, d)
    n_gate = 2 * NH_M
    b_if = jnp.pad(jnp.concatenate([b_i, b_f], axis=-1), ((0, 0), (0, GATE_LANES - n_gate)))
    b_if = b_if.reshape(depth, 1, GATE_LANES)
    wkv = w_mkv.astype(BF16)
    wb = w_branch.astype(BF16)
    wo = w_out.astype(BF16)
    conv_b3 = conv_b.reshape(depth, 1, d)
    ng3 = mlstm_norm_g.reshape(depth, 1, d)
    lng3 = sgu_ln_g.reshape(depth, 1, d)
    lnb3 = sgu_ln_b.reshape(depth, 1, d)
    bs4 = sgu_b_s.reshape(depth, SGU_GROUPS, SGU_CHUNK, 1)

    h = x.reshape(n, d)
    for l in range(depth):
        h = _ffn(h, norms, l, 0, 1, wgu, wd, 0)
        kt, v = _memkv(mem, norms, l, wkv)
        y_m, z = _mixa(h, b, s, norms, l, w_mlstm, w_rest, w_if, conv_w, conv_b3, b_if, ng3)
        h = _mixb(h, z, y_m, kt, v, s, lng3, lnb3, sgu_w_s, bs4, wb, wo, norms, l)
        h = _ffn(h, norms, l, 4, 5, wgu, wd, 1)
    return h.reshape(b, s, d)
```

```python
import functools

import jax
import jax.numpy as jnp
from jax import lax
from jax.experimental import pallas as pl
from jax.experimental.pallas import tpu as pltpu

F32 = jnp.float32
BF16 = jnp.bfloat16

EPS = 1e-6
NEG_INIT = -1e30
NH_M = 4
NH_X = 4
SGU_GROUPS = 4
SGU_CHUNK = 128
CONV_K = 4
N_BRANCH = 3
N_NORMS = 7
N_Z_MLSTM = 3
N_Z_REST = 6
GATE_LANES = 128
HIST = 8

VMEM_LIMIT_V7X = 56 * 1024 * 1024

FFN_TM = 1024
FFN_RS = 512
FFN_CW = 256
MIX_TM = 512
MLSTM_L = 256


def _params(semantics):
    return pltpu.CompilerParams(dimension_semantics=semantics, vmem_limit_bytes=VMEM_LIMIT_V7X)


def _resident(block_shape, index_map):
    return pl.BlockSpec(block_shape, index_map, pipeline_mode=pl.Buffered(1))


def _rms(x, g):
    return x * lax.rsqrt(jnp.mean(x * x, axis=-1, keepdims=True) + EPS) * g


def _dot(a, b):
    return jnp.dot(a, b, preferred_element_type=F32)


LOG2E = 1.4426950408889634
GELU_K = 2.0 * 0.7978845608028654
GELU_C = 0.044715


def _sigmoid(x):
    return 1.0 / (1.0 + jnp.exp2(x * (-LOG2E)))


def _gelu_tanh(x):
    t = x * ((-GELU_K * LOG2E) + (-GELU_K * GELU_C * LOG2E) * (x * x))
    return x / (1.0 + jnp.exp2(t))


def _ffn_body(x_ref, gpre_ref, gpost_ref, wgu_ref, wd_ref, o_ref, xn_ref, hid_ref, *, dff, cw, rs):
    n_sub = x_ref.shape[0] // rs
    nch = dff // cw

    def rows(r):
        return slice(r * rs, (r + 1) * rs)

    def normalise(r):
        xn_ref[rows(r), :] = _rms(x_ref[rows(r), :], gpre_ref[...]).astype(BF16)

    def hidden_chunk(r, c):
        xn = xn_ref[rows(r), :]
        a = _dot(xn, wgu_ref[:, c * cw:(c + 1) * cw])
        b = _dot(xn, wgu_ref[:, dff + c * cw:dff + (c + 1) * cw])
        hid_ref[rows(r), c * cw:(c + 1) * cw] = (a * jax.nn.sigmoid(a) * b).astype(BF16)

    def finish(r):
        f = _dot(hid_ref[rows(r), :], wd_ref[...])
        o_ref[rows(r), :] = x_ref[rows(r), :] + 0.5 * _rms(f, gpost_ref[...])

    normalise(0)
    for r in range(n_sub):
        for c in range(nch):
            hidden_chunk(r, c)
            if c == 1 and r + 1 < n_sub:
                normalise(r + 1)
            if c == 1 and r > 0:
                finish(r - 1)
    finish(n_sub - 1)


def _ffn(h, norms, l, k_pre, k_post, wgu, wd, ffn_idx):
    n, d = h.shape
    dff = wd.shape[2]
    tm = min(FFN_TM, n)
    return pl.pallas_call(
        functools.partial(_ffn_body, dff=dff, cw=FFN_CW, rs=min(FFN_RS, tm)),
        grid=(n // tm,),
        in_specs=[
            pl.BlockSpec((tm, d), lambda i: (i, 0)),
            _resident((None, 1, d), lambda i: (l * N_NORMS + k_pre, 0, 0)),
            _resident((None, 1, d), lambda i: (l * N_NORMS + k_post, 0, 0)),
            _resident((None, None, d, 2 * dff), lambda i: (l, ffn_idx, 0, 0)),
            _resident((None, None, dff, d), lambda i: (l, ffn_idx, 0, 0)),
        ],
        out_specs=pl.BlockSpec((tm, d), lambda i: (i, 0)),
        out_shape=jax.ShapeDtypeStruct((n, d), F32),
        scratch_shapes=[pltpu.VMEM((tm, d), BF16), pltpu.VMEM((tm, dff), BF16)],
        compiler_params=_params(("parallel",)),
        name="ffn",
    )(h, norms, norms, wgu, wd)


def _memkv_body(mem_ref, g_ref, wkv_ref, kt_ref, v_ref, *, d, scale):
    mn = _rms(mem_ref[...], g_ref[...]).astype(BF16)
    k = _dot(mn, wkv_ref[:, :d]) * scale
    kt_ref[...] = k.T.astype(BF16)
    v_ref[...] = _dot(mn, wkv_ref[:, d:]).astype(BF16)


def _memkv(mem, norms, l, wkv):
    b, m, d = mem.shape
    scale = float((d // NH_X) ** -0.5)
    return pl.pallas_call(
        functools.partial(_memkv_body, d=d, scale=scale),
        grid=(b,),
        in_specs=[
            pl.BlockSpec((None, m, d), lambda i: (i, 0, 0)),
            _resident((None, 1, d), lambda i: (l * N_NORMS + 6, 0, 0)),
            _resident((None, d, 2 * d), lambda i: (l, 0, 0)),
        ],
        out_specs=[
            pl.BlockSpec((None, d, m), lambda i: (i, 0, 0)),
            pl.BlockSpec((None, m, d), lambda i: (i, 0, 0)),
        ],
        out_shape=[
            jax.ShapeDtypeStruct((b, d, m), BF16),
            jax.ShapeDtypeStruct((b, m, d), BF16),
        ],
        compiler_params=_params(("parallel",)),
        name="memkv",
    )(mem, norms, wkv)


def _log_sigmoid(x):
    return -(jnp.maximum(-x, 0.0) + jnp.log1p(jnp.exp(-jnp.abs(x))))


def _split3(x):
    hi = x.astype(BF16)
    r1 = x - hi.astype(F32)
    mid = r1.astype(BF16)
    lo = (r1 - mid.astype(F32)).astype(BF16)
    return hi, mid, lo


def _mixa_body(h_ref, g_ref, wm_ref, wr_ref, wif_ref, cw_ref, cb_ref, bif_ref, ng_ref,
               y_ref, z_ref,
               xn_ref, xs_ref, q_ref, k_ref, v_ref, o_ref, gc_ref, gr_ref, c_ref, n_ref, m_ref,
               *, tm, chunk, d, dqk, dv):
    i = pl.program_id(1)
    nqk = NH_M * dqk
    n_chunks = tm // chunk
    zc = wr_ref.shape[-1] // n_chunks
    zw = zc // NH_M

    @pl.when(i == 0)
    def _():
        c_ref[...] = jnp.zeros_like(c_ref)
        n_ref[...] = jnp.zeros_like(n_ref)
        m_ref[...] = jnp.full_like(m_ref, NEG_INIT)
        xs_ref[0:HIST, :] = jnp.zeros((HIST, d), F32)

    @pl.when(i > 0)
    def _():
        xs_ref[0:HIST, :] = xs_ref[tm:tm + HIST, :]

    xn_ref[...] = _rms(h_ref[...], g_ref[...]).astype(BF16)
    xs_ref[HIST:, :] = _dot(xn_ref[...], wm_ref[:, 0:d])
    gp = _dot(xn_ref[...], wif_ref[...]) + bif_ref[...]
    v_ref[...] = _dot(xn_ref[...], wm_ref[:, d:2 * d]).astype(BF16)
    o_ref[...] = _dot(xn_ref[...], wm_ref[:, 2 * d:3 * d]).astype(BF16)

    lane = lax.broadcasted_iota(jnp.int32, gp.shape, 1)
    gates = jnp.where(lane < NH_M, gp, _log_sigmoid(gp))
    gc_ref[...] = gates
    gates_t = gates.T
    for c in range(n_chunks):
        gr_ref[c] = gates_t[0:2 * NH_M, c * chunk:(c + 1) * chunk]

    r_i = lax.broadcasted_iota(jnp.int32, (chunk, chunk), 0)
    c_i = lax.broadcasted_iota(jnp.int32, (chunk, chunk), 1)
    causal = r_i >= c_i

    def z_piece(c, p):
        cols = slice(p * zw, (p + 1) * zw)
        z_ref[c, :, cols] = _dot(xn_ref[...], wr_ref[:, c * zc + p * zw:c * zc + (p + 1) * zw]).astype(BF16)

    def chunk_body(c):
        rows = slice(c * chunk, (c + 1) * chunk)

        conv = cb_ref[...]
        for j in range(CONV_K):
            conv = conv + cw_ref[j:j + 1, :] * xs_ref[pl.ds(c * chunk + HIST - (CONV_K - 1) + j, chunk), :]
        qk = conv * _sigmoid(conv)
        q_ref[rows, :] = qk[:, :nqk]
        k_ref[rows, :] = qk[:, nqk:] * float(dqk ** -0.5)

        g_c = gc_ref[rows, :]
        g_r = gr_ref[c]
        tril = jnp.where(causal, 1.0, 0.0).astype(BF16)
        triu = jnp.where(r_i <= c_i, 1.0, 0.0).astype(BF16)
        cum_c = sum(_dot(tril, p) for p in _split3(g_c))
        cum_r = sum(_dot(p, triu) for p in _split3(g_r))
        for h in range(NH_M):
            z_piece(c, h)

            qc = q_ref[rows, h * dqk:(h + 1) * dqk]
            kc = k_ref[rows, h * dqk:(h + 1) * dqk]
            vc = v_ref[rows, h * dv:(h + 1) * dv]
            qb = qc.astype(BF16)
            i_col = g_c[:, h:h + 1]
            b_col = cum_c[:, NH_M + h:NH_M + h + 1]
            i_row = g_r[h:h + 1, :]
            b_row = cum_r[NH_M + h:NH_M + h + 1, :]
            m_prev = m_ref[h, :, 0:1]
            g_tot = b_col[chunk - 1:chunk, :]

            log_d = jnp.where(causal, b_col + (i_row - b_row), -jnp.inf)
            m_inter = b_col + m_prev
            m_t = jnp.maximum(m_inter, jnp.max(log_d, axis=-1, keepdims=True))
            d_mat = jnp.exp(log_d - m_t)
            s = lax.dot_general(qb, kc.astype(BF16), (((1,), (1,)), ((), ())),
                                preferred_element_type=F32) * d_mat
            inter = jnp.exp(m_inter - m_t)
            num = _dot(s.astype(BF16), vc) + inter * _dot(qb, c_ref[h].astype(BF16))
            den = (jnp.sum(s, axis=-1, keepdims=True)
                   + inter * jnp.sum(qc * n_ref[h], axis=-1, keepdims=True))
            hh = num / jnp.maximum(jnp.abs(den), jnp.exp(-m_t))

            a_col = g_tot - b_col + i_col
            m_new = jnp.maximum(g_tot + m_prev, jnp.max(a_col, axis=0, keepdims=True))
            wk = jnp.exp(a_col - m_new) * kc
            decay = jnp.exp(g_tot + m_prev - m_new)
            c_ref[h] = decay * c_ref[h] + lax.dot_general(
                wk.astype(BF16), vc, (((0,), (0,)), ((), ())), preferred_element_type=F32)
            n_ref[h] = decay * n_ref[h] + jnp.sum(wk, axis=0, keepdims=True)
            m_ref[h] = jnp.broadcast_to(m_new, m_ref.shape[1:])

            mu = jnp.mean(hh, axis=-1, keepdims=True)
            hc = hh - mu
            hn = hc * lax.rsqrt(jnp.mean(hc * hc, axis=-1, keepdims=True) + EPS)
            hn = hn * ng_ref[:, h * dv:(h + 1) * dv]
            og = _sigmoid(o_ref[rows, h * dv:(h + 1) * dv].astype(F32))
            y_ref[rows, h * dv:(h + 1) * dv] = (hn * og).astype(BF16)

    for c in range(n_chunks):
        chunk_body(c)


def _mixa(h, b, s, norms, l, w_mlstm, w_rest, w_if, conv_w, conv_b, b_if, norm_g):
    n, d = h.shape
    dv = d // NH_M
    dqk = dv // 2
    tm = min(MIX_TM, s)
    chunk = min(MLSTM_L, tm)
    nt = s // tm
    n_chunks = tm // chunk
    zc = w_rest.shape[2] // n_chunks
    assert zc * n_chunks == w_rest.shape[2] and zc % (NH_M * 128) == 0
    return pl.pallas_call(
        functools.partial(_mixa_body, tm=tm, chunk=chunk, d=d, dqk=dqk, dv=dv),
        grid=(b, nt),
        in_specs=[
            pl.BlockSpec((tm, d), lambda bi, i: (bi * nt + i, 0)),
            _resident((None, 1, d), lambda bi, i: (l * N_NORMS + 2, 0, 0)),
            _resident((None, d, N_Z_MLSTM * d), lambda bi, i: (l, 0, 0)),
            _resident((None, d, n_chunks * zc), lambda bi, i: (l, 0, 0)),
            _resident((None, d, GATE_LANES), lambda bi, i: (l, 0, 0)),
            _resident((None, CONV_K, d), lambda bi, i: (l, 0, 0)),
            _resident((None, 1, d), lambda bi, i: (l, 0, 0)),
            _resident((None, 1, GATE_LANES), lambda bi, i: (l, 0, 0)),
            _resident((None, 1, d), lambda bi, i: (l, 0, 0)),
        ],
        out_specs=[
            pl.BlockSpec((tm, d), lambda bi, i: (bi * nt + i, 0)),
            pl.BlockSpec((n_chunks, tm, zc), lambda bi, i: (0, bi * nt + i, 0)),
        ],
        out_shape=[
            jax.ShapeDtypeStruct((n, d), BF16),
            jax.ShapeDtypeStruct((n_chunks, n, zc), BF16),
        ],
        scratch_shapes=[
            pltpu.VMEM((tm, d), BF16),
            pltpu.VMEM((tm + HIST, d), F32),
            pltpu.VMEM((tm, NH_M * dqk), F32),
            pltpu.VMEM((tm, NH_M * dqk), F32),
            pltpu.VMEM((tm, d), BF16),
            pltpu.VMEM((tm, d), BF16),
            pltpu.VMEM((tm, GATE_LANES), F32),
            pltpu.VMEM((n_chunks, 2 * NH_M, chunk), F32),
            pltpu.VMEM((NH_M, dqk, dv), F32),
            pltpu.VMEM((NH_M, 1, dqk), F32),
            pltpu.VMEM((NH_M, 1, GATE_LANES), F32),
        ],
        compiler_params=_params(("arbitrary", "arbitrary")),
        name="mixa",
    )(h, norms, w_mlstm, w_rest, w_if, conv_w, conv_b, b_if, norm_g)


def _mixb_body(zf_ref, kt_ref, v_ref, zg_ref, ym_ref, h_ref, lng_ref, lnb_ref, ws_ref, bs_ref, wb_ref, wo_ref,
               g_ref, o_ref, ys_ref, yx_ref, ysp_ref, yxp_ref, mg_ref, yo_ref, *, tm, d):
    k = pl.program_id(0)
    dg = d // SGU_GROUPS
    dh = d // NH_X
    zu, zv, zq = 0, d, 2 * d

    @pl.when(k == 0)
    def _():
        ys_ref[...] = jnp.zeros((tm, d), BF16)
        yx_ref[...] = jnp.zeros((tm, d), BF16)

    ysp_ref[...] = ys_ref[...]
    yxp_ref[...] = yx_ref[...]

    r_i = lax.broadcasted_iota(jnp.int32, (SGU_CHUNK, SGU_CHUNK), 0)
    c_i = lax.broadcasted_iota(jnp.int32, (SGU_CHUNK, SGU_CHUNK), 1)
    causal = r_i >= c_i

    def sgu_group(g):
        cols = slice(g * dg, (g + 1) * dg)
        v = _gelu_tanh(zf_ref[:, zv + g * dg:zv + (g + 1) * dg].astype(F32))
        mu = jnp.mean(v, axis=-1, keepdims=True)
        vc = v - mu
        vn = vc * lax.rsqrt(jnp.mean(vc * vc, axis=-1, keepdims=True) + EPS) * lng_ref[:, cols]
        vn = (vn + lnb_ref[:, cols]).astype(BF16)
        w = jnp.where(causal, ws_ref[g], 0.0).astype(BF16)
        for c in range(tm // SGU_CHUNK):
            rows = slice(c * SGU_CHUNK, (c + 1) * SGU_CHUNK)
            mixed = _dot(w, vn[rows, :]) + bs_ref[g]
            u = _gelu_tanh(zf_ref[rows, zu + g * dg:zu + (g + 1) * dg].astype(F32))
            ys_ref[rows, cols] = (u * mixed).astype(BF16)

    def xattn_head(h):
        cols = slice(h * dh, (h + 1) * dh)
        s = _dot(zf_ref[:, zq + h * dh:zq + (h + 1) * dh], kt_ref[cols, :])
        p = jnp.exp(s - jnp.max(s, axis=-1, keepdims=True))
        p = p / jnp.sum(p, axis=-1, keepdims=True)
        yx_ref[:, cols] = _dot(p.astype(BF16), v_ref[:, cols]).astype(BF16)

    half = d // 2

    def branch_half(j, y_ref, hf):
        cols = slice(hf * half, (hf + 1) * half)
        gate = _sigmoid(zg_ref[:, j * d + hf * half:j * d + (hf + 1) * half].astype(F32))
        term = gate * _dot(y_ref[...], wb_ref[j, :, cols])
        if j == 0:
            mg_ref[:, cols] = term
        else:
            mg_ref[:, cols] += term

    def out_half(hf):
        cols = slice(hf * half, (hf + 1) * half)
        yo_ref[:, cols] = _dot(mg_ref[...].astype(BF16), wo_ref[:, cols])

    sgu_group(0)
    branch_half(0, ym_ref, 0)
    sgu_group(1)
    branch_half(0, ym_ref, 1)
    sgu_group(2)
    branch_half(1, ysp_ref, 0)
    sgu_group(3)
    branch_half(1, ysp_ref, 1)
    xattn_head(0)
    branch_half(2, yxp_ref, 0)
    xattn_head(1)
    branch_half(2, yxp_ref, 1)
    xattn_head(2)
    out_half(0)
    xattn_head(3)
    out_half(1)
    o_ref[...] = h_ref[...] + _rms(yo_ref[...], g_ref[...])


def _mixb(h, z, y_m, kt, v, s, ln_g, ln_b, w_s, b_s, w_branch, w_out, norms, l):
    n, d = h.shape
    m = v.shape[1]
    tm = min(MIX_TM, s)
    nt = s // tm
    last = n // tm - 1
    assert z.shape[0] == 2 and z.shape[2] == N_BRANCH * d

    def cur(k):
        return jnp.minimum(k, last)

    def prev(k):
        return jnp.maximum(k - 1, 0)

    tok_prev = pl.BlockSpec((tm, d), lambda k: (prev(k), 0))
    return pl.pallas_call(
        functools.partial(_mixb_body, tm=tm, d=d),
        grid=(n // tm + 1,),
        in_specs=[
            pl.BlockSpec((None, tm, N_BRANCH * d), lambda k: (0, cur(k), 0)),
            pl.BlockSpec((None, d, m), lambda k: (cur(k) // nt, 0, 0)),
            pl.BlockSpec((None, m, d), lambda k: (cur(k) // nt, 0, 0)),
            pl.BlockSpec((None, tm, N_BRANCH * d), lambda k: (1, prev(k), 0)),
            tok_prev,
            tok_prev,
            _resident((None, 1, d), lambda k: (l, 0, 0)),
            _resident((None, 1, d), lambda k: (l, 0, 0)),
            _resident((None, SGU_GROUPS, SGU_CHUNK, SGU_CHUNK), lambda k: (l, 0, 0, 0)),
            _resident((None, SGU_GROUPS, SGU_CHUNK, 1), lambda k: (l, 0, 0, 0)),
            _resident((None, N_BRANCH, d, d), lambda k: (l, 0, 0, 0)),
            _resident((None, d, d), lambda k: (l, 0, 0)),
            _resident((None, 1, d), lambda k: (l * N_NORMS + 3, 0, 0)),
        ],
        out_specs=tok_prev,
        out_shape=jax.ShapeDtypeStruct((n, d), F32),
        scratch_shapes=[pltpu.VMEM((tm, d), BF16) for _ in range(4)] + [pltpu.VMEM((tm, d), F32) for _ in range(2)],
        compiler_params=_params(("arbitrary",)),
        name="mixb",
    )(z, kt, v, z, y_m, h, ln_g, ln_b, w_s, b_s, w_branch, w_out, norms)


def kernel(x, mem, norm_g, w_ffn_gu, w_ffn_down, w_in, conv_w, conv_b, b_i, b_f, mlstm_norm_g,
           sgu_ln_g, sgu_ln_b, sgu_w_s, sgu_b_s, w_mkv, w_branch, w_out):
    b, s, d = x.shape
    depth = norm_g.shape[0]
    dff = w_ffn_down.shape[2]
    n = b * s
    assert s % MIX_TM == 0 or s < MIX_TM
    assert dff % FFN_CW == 0

    norms = norm_g.reshape(depth * N_NORMS, 1, d)
    wgu = w_ffn_gu.astype(BF16)
    wd = w_ffn_down.astype(BF16)
    n_gate = 2 * NH_M
    c_if = N_Z_MLSTM * d
    w_in_bf = w_in.astype(BF16)
    w_mlstm = w_in_bf[:, :, :c_if]
    w_rest = w_in_bf[:, :, c_if + n_gate:]
    w_if = w_in_bf[:, :, c_if:c_if + GATE_LANES]
    b_if = jnp.pad(jnp.concatenate([b_i, b_f], axis=-1), ((0, 0), (0, GATE_LANES - n_gate)))
    b_if = b_if.reshape(depth, 1, GATE_LANES)
    wkv = w_mkv.astype(BF16)
    wb = w_branch.astype(BF16)
    wo = w_out.astype(BF16)
    conv_b3 = conv_b.reshape(depth, 1, d)
    ng3 = mlstm_norm_g.reshape(depth, 1, d)
    lng3 = sgu_ln_g.reshape(depth, 1, d)
    lnb3 = sgu_ln_b.reshape(depth, 1, d)
    bs4 = sgu_b_s.reshape(depth, SGU_GROUPS, SGU_CHUNK, 1)

    h = x.reshape(n, d)
    for l in range(depth):
        h = _ffn(h, norms, l, 0, 1, wgu, wd, 0)
        kt, v = _memkv(mem, norms, l, wkv)
        y_m, z = _mixa(h, b, s, norms, l, w_mlstm, w_rest, w_if, conv_w, conv_b3, b_if, ng3)
        h = _mixb(h, z, y_m, kt, v, s, lng3, lnb3, sgu_w_s, bs4, wb, wo, norms, l)
        h = _ffn(h, norms, l, 4, 5, wgu, wd, 1)
    return h.reshape(b, s, d)
```

```python
import functools

import jax
import jax.numpy as jnp
from jax import lax
from jax.experimental import pallas as pl
from jax.experimental.pallas import tpu as pltpu

F32 = jnp.float32
BF16 = jnp.bfloat16

EPS = 1e-6
NEG_INIT = -1e30
NH_M = 4
NH_X = 4
SGU_GROUPS = 4
SGU_CHUNK = 128
CONV_K = 4
N_BRANCH = 3
N_NORMS = 7
N_Z_MLSTM = 3
N_Z_REST = 6
GATE_LANES = 128
HIST = 8

VMEM_LIMIT_V7X = 56 * 1024 * 1024

FFN_TM = 1024
FFN_RS = 512
FFN_CW = 256
MIX_TM = 512
MLSTM_L = 256


def _params(semantics):
    return pltpu.CompilerParams(dimension_semantics=semantics, vmem_limit_bytes=VMEM_LIMIT_V7X)


def _resident(block_shape, index_map):
    return pl.BlockSpec(block_shape, index_map, pipeline_mode=pl.Buffered(1))


def _rms(x, g):
    return x * lax.rsqrt(jnp.mean(x * x, axis=-1, keepdims=True) + EPS) * g


def _dot(a, b):
    return jnp.dot(a, b, preferred_element_type=F32)


LOG2E = 1.4426950408889634
GELU_K = 2.0 * 0.7978845608028654
GELU_C = 0.044715


def _sigmoid(x):
    return 1.0 / (1.0 + jnp.exp2(x * (-LOG2E)))


def _gelu_tanh(x):
    t = x * ((-GELU_K * LOG2E) + (-GELU_K * GELU_C * LOG2E) * (x * x))
    return x / (1.0 + jnp.exp2(t))


def _ffn_body(x_ref, gpre_ref, gpost_ref, wgu_ref, wd_ref, o_ref, xn_ref, hid_ref, *, dff, cw, rs):
    n_sub = x_ref.shape[0] // rs
    nch = dff // cw

    def rows(r):
        return slice(r * rs, (r + 1) * rs)

    def normalise(r):
        xn_ref[rows(r), :] = _rms(x_ref[rows(r), :], gpre_ref[...]).astype(BF16)

    def hidden_chunk(r, c):
        xn = xn_ref[rows(r), :]
        a = _dot(xn, wgu_ref[:, c * cw:(c + 1) * cw])
        b = _dot(xn, wgu_ref[:, dff + c * cw:dff + (c + 1) * cw])
        hid_ref[rows(r), c * cw:(c + 1) * cw] = (a * jax.nn.sigmoid(a) * b).astype(BF16)

    def finish(r):
        f = _dot(hid_ref[rows(r), :], wd_ref[...])
        o_ref[rows(r), :] = x_ref[rows(r), :] + 0.5 * _rms(f, gpost_ref[...])

    normalise(0)
    for r in range(n_sub):
        for c in range(nch):
            hidden_chunk(r, c)
            if c == 1 and r + 1 < n_sub:
                normalise(r + 1)
            if c == 1 and r > 0:
                finish(r - 1)
    finish(n_sub - 1)


def _ffn(h, norms, l, k_pre, k_post, wgu, wd, ffn_idx):
    n, d = h.shape
    dff = wd.shape[2]
    tm = min(FFN_TM, n)
    return pl.pallas_call(
        functools.partial(_ffn_body, dff=dff, cw=FFN_CW, rs=min(FFN_RS, tm)),
        grid=(n // tm,),
        in_specs=[
            pl.BlockSpec((tm, d), lambda i: (i, 0)),
            _resident((None, 1, d), lambda i: (l * N_NORMS + k_pre, 0, 0)),
            _resident((None, 1, d), lambda i: (l * N_NORMS + k_post, 0, 0)),
            _resident((None, None, d, 2 * dff), lambda i: (l, ffn_idx, 0, 0)),
            _resident((None, None, dff, d), lambda i: (l, ffn_idx, 0, 0)),
        ],
        out_specs=pl.BlockSpec((tm, d), lambda i: (i, 0)),
        out_shape=jax.ShapeDtypeStruct((n, d), F32),
        scratch_shapes=[pltpu.VMEM((tm, d), BF16), pltpu.VMEM((tm, dff), BF16)],
        compiler_params=_params(("parallel",)),
        name="ffn",
    )(h, norms, norms, wgu, wd)


def _memkv_body(mem_ref, g_ref, wkv_ref, kt_ref, v_ref, *, d, scale):
    mn = _rms(mem_ref[...], g_ref[...]).astype(BF16)
    k = _dot(mn, wkv_ref[:, :d]) * scale
    kt_ref[...] = k.T.astype(BF16)
    v_ref[...] = _dot(mn, wkv_ref[:, d:]).astype(BF16)


def _memkv(mem, norms, l, wkv):
    b, m, d = mem.shape
    scale = float((d // NH_X) ** -0.5)
    return pl.pallas_call(
        functools.partial(_memkv_body, d=d, scale=scale),
        grid=(b,),
        in_specs=[
            pl.BlockSpec((None, m, d), lambda i: (i, 0, 0)),
            _resident((None, 1, d), lambda i: (l * N_NORMS + 6, 0, 0)),
            _resident((None, d, 2 * d), lambda i: (l, 0, 0)),
        ],
        out_specs=[
            pl.BlockSpec((None, d, m), lambda i: (i, 0, 0)),
            pl.BlockSpec((None, m, d), lambda i: (i, 0, 0)),
        ],
        out_shape=[
            jax.ShapeDtypeStruct((b, d, m), BF16),
            jax.ShapeDtypeStruct((b, m, d), BF16),
        ],
        compiler_params=_params(("parallel",)),
        name="memkv",
    )(mem, norms, wkv)


def _log_sigmoid(x):
    return -(jnp.maximum(-x, 0.0) + jnp.log1p(jnp.exp(-jnp.abs(x))))


def _split3(x):
    hi = x.astype(BF16)
    r1 = x - hi.astype(F32)
    mid = r1.astype(BF16)
    lo = (r1 - mid.astype(F32)).astype(BF16)
    return hi, mid, lo


def _mixa_body(h_ref, g_ref, wm_ref, wr_ref, wif_ref, cw_ref, cb_ref, bif_ref, ng_ref,
               y_ref, z_ref,
               xn_ref, xs_ref, q_ref, k_ref, v_ref, o_ref, gc_ref, gr_ref, c_ref, n_ref, m_ref,
               *, tm, chunk, d, dqk, dv):
    i = pl.program_id(1)
    nqk = NH_M * dqk
    n_chunks = tm // chunk
    zc = wr_ref.shape[-1] // n_chunks
    zw = zc // NH_M

    @pl.when(i == 0)
    def _():
        c_ref[...] = jnp.zeros_like(c_ref)
        n_ref[...] = jnp.zeros_like(n_ref)
        m_ref[...] = jnp.full_like(m_ref, NEG_INIT)
        xs_ref[tm:tm + HIST, :] = jnp.zeros((HIST, d), F32)

    xs_ref[0:HIST, :] = xs_ref[tm:tm + HIST, :]

    xn_ref[...] = _rms(h_ref[...], g_ref[...]).astype(BF16)
    xs_ref[HIST:, :] = _dot(xn_ref[...], wm_ref[:, 0:d])
    gp = _dot(xn_ref[...], wif_ref[...]) + bif_ref[...]
    v_ref[...] = _dot(xn_ref[...], wm_ref[:, d:2 * d]).astype(BF16)
    o_ref[...] = _dot(xn_ref[...], wm_ref[:, 2 * d:3 * d]).astype(BF16)

    lane = lax.broadcasted_iota(jnp.int32, gp.shape, 1)
    gates = jnp.where(lane < NH_M, gp, _log_sigmoid(gp))
    gc_ref[...] = gates
    gates_t = gates.T
    for c in range(n_chunks):
        gr_ref[c] = gates_t[0:2 * NH_M, c * chunk:(c + 1) * chunk]

    r_i = lax.broadcasted_iota(jnp.int32, (chunk, chunk), 0)
    c_i = lax.broadcasted_iota(jnp.int32, (chunk, chunk), 1)
    causal = r_i >= c_i

    def z_piece(c, p):
        cols = slice(p * zw, (p + 1) * zw)
        z_ref[c, :, cols] = _dot(xn_ref[...], wr_ref[:, c * zc + p * zw:c * zc + (p + 1) * zw]).astype(BF16)

    def chunk_body(c):
        rows = slice(c * chunk, (c + 1) * chunk)

        conv = cb_ref[...]
        for j in range(CONV_K):
            conv = conv + cw_ref[j:j + 1, :] * xs_ref[pl.ds(c * chunk + HIST - (CONV_K - 1) + j, chunk), :]
        qk = conv * _sigmoid(conv)
        q_ref[rows, :] = qk[:, :nqk]
        k_ref[rows, :] = qk[:, nqk:] * float(dqk ** -0.5)

        g_c = gc_ref[rows, :]
        g_r = gr_ref[c]
        tril = jnp.where(causal, 1.0, 0.0).astype(BF16)
        triu = jnp.where(r_i <= c_i, 1.0, 0.0).astype(BF16)
        cum_c = sum(_dot(tril, p) for p in _split3(g_c))
        cum_r = sum(_dot(p, triu) for p in _split3(g_r))
        for h in range(NH_M):
            z_piece(c, h)

            qc = q_ref[rows, h * dqk:(h + 1) * dqk]
            kc = k_ref[rows, h * dqk:(h + 1) * dqk]
            vc = v_ref[rows, h * dv:(h + 1) * dv]
            qb = qc.astype(BF16)
            i_col = g_c[:, h:h + 1]
            b_col = cum_c[:, NH_M + h:NH_M + h + 1]
            i_row = g_r[h:h + 1, :]
            b_row = cum_r[NH_M + h:NH_M + h + 1, :]
            m_prev = m_ref[h, :, 0:1]
            g_tot = b_col[chunk - 1:chunk, :]

            log_d = jnp.where(causal, b_col + (i_row - b_row), -jnp.inf)
            m_inter = b_col + m_prev
            m_t = jnp.maximum(m_inter, jnp.max(log_d, axis=-1, keepdims=True))
            d_mat = jnp.exp(log_d - m_t)
            s = lax.dot_general(qb, kc.astype(BF16), (((1,), (1,)), ((), ())),
                                preferred_element_type=F32) * d_mat
            inter = jnp.exp(m_inter - m_t)
            num = _dot(s.astype(BF16), vc) + inter * _dot(qb, c_ref[h].astype(BF16))
            den = (jnp.sum(s, axis=-1, keepdims=True)
                   + inter * jnp.sum(qc * n_ref[h], axis=-1, keepdims=True))
            hh = num / jnp.maximum(jnp.abs(den), jnp.exp(-m_t))

            a_col = g_tot - b_col + i_col
            m_new = jnp.maximum(g_tot + m_prev, jnp.max(a_col, axis=0, keepdims=True))
            wk = jnp.exp(a_col - m_new) * kc
            decay = jnp.exp(g_tot + m_prev - m_new)
            c_ref[h] = decay * c_ref[h] + lax.dot_general(
                wk.astype(BF16), vc, (((0,), (0,)), ((), ())), preferred_element_type=F32)
            n_ref[h] = decay * n_ref[h] + jnp.sum(wk, axis=0, keepdims=True)
            m_ref[h] = jnp.broadcast_to(m_new, m_ref.shape[1:])

            mu = jnp.mean(hh, axis=-1, keepdims=True)
            hc = hh - mu
            hn = hc * lax.rsqrt(jnp.mean(hc * hc, axis=-1, keepdims=True) + EPS)
            hn = hn * ng_ref[:, h * dv:(h + 1) * dv]
            og = _sigmoid(o_ref[rows, h * dv:(h + 1) * dv].astype(F32))
            y_ref[rows, h * dv:(h + 1) * dv] = (hn * og).astype(BF16)

    for c in range(n_chunks):
        chunk_body(c)


def _mixa(h, b, s, norms, l, w_mlstm, w_rest, w_if, conv_w, conv_b, b_if, norm_g):
    n, d = h.shape
    dv = d // NH_M
    dqk = dv // 2
    tm = min(MIX_TM, s)
    chunk = min(MLSTM_L, tm)
    nt = s // tm
    n_chunks = tm // chunk
    zc = w_rest.shape[2] // n_chunks
    assert zc * n_chunks == w_rest.shape[2] and zc % (NH_M * 128) == 0
    return pl.pallas_call(
        functools.partial(_mixa_body, tm=tm, chunk=chunk, d=d, dqk=dqk, dv=dv),
        grid=(b, nt),
        in_specs=[
            pl.BlockSpec((tm, d), lambda bi, i: (bi * nt + i, 0)),
            _resident((None, 1, d), lambda bi, i: (l * N_NORMS + 2, 0, 0)),
            _resident((None, d, N_Z_MLSTM * d), lambda bi, i: (l, 0, 0)),
            _resident((None, d, n_chunks * zc), lambda bi, i: (l, 0, 0)),
            _resident((None, d, GATE_LANES), lambda bi, i: (l, 0, 0)),
            _resident((None, CONV_K, d), lambda bi, i: (l, 0, 0)),
            _resident((None, 1, d), lambda bi, i: (l, 0, 0)),
            _resident((None, 1, GATE_LANES), lambda bi, i: (l, 0, 0)),
            _resident((None, 1, d), lambda bi, i: (l, 0, 0)),
        ],
        out_specs=[
            pl.BlockSpec((tm, d), lambda bi, i: (bi * nt + i, 0)),
            pl.BlockSpec((n_chunks, tm, zc), lambda bi, i: (0, bi * nt + i, 0)),
        ],
        out_shape=[
            jax.ShapeDtypeStruct((n, d), BF16),
            jax.ShapeDtypeStruct((n_chunks, n, zc), BF16),
        ],
        scratch_shapes=[
            pltpu.VMEM((tm, d), BF16),
            pltpu.VMEM((tm + HIST, d), F32),
            pltpu.VMEM((tm, NH_M * dqk), F32),
            pltpu.VMEM((tm, NH_M * dqk), F32),
            pltpu.VMEM((tm, d), BF16),
            pltpu.VMEM((tm, d), BF16),
            pltpu.VMEM((tm, GATE_LANES), F32),
            pltpu.VMEM((n_chunks, 2 * NH_M, chunk), F32),
            pltpu.VMEM((NH_M, dqk, dv), F32),
            pltpu.VMEM((NH_M, 1, dqk), F32),
            pltpu.VMEM((NH_M, 1, GATE_LANES), F32),
        ],
        compiler_params=_params(("arbitrary", "arbitrary")),
        name="mixa",
    )(h, norms, w_mlstm, w_rest, w_if, conv_w, conv_b, b_if, norm_g)


def _mixb_body(zf_ref, kt_ref, v_ref, zg_ref, ym_ref, h_ref, lng_ref, lnb_ref, ws_ref, bs_ref, wb_ref, wo_ref,
               g_ref, o_ref, ys_ref, yx_ref, ysp_ref, yxp_ref, mg_ref, yo_ref, *, tm, d):
    k = pl.program_id(0)
    dg = d // SGU_GROUPS
    dh = d // NH_X
    zu, zv, zq = 0, d, 2 * d

    @pl.when(k == 0)
    def _():
        ys_ref[...] = jnp.zeros((tm, d), BF16)
        yx_ref[...] = jnp.zeros((tm, d), BF16)

    ysp_ref[...] = ys_ref[...]
    yxp_ref[...] = yx_ref[...]

    r_i = lax.broadcasted_iota(jnp.int32, (SGU_CHUNK, SGU_CHUNK), 0)
    c_i = lax.broadcasted_iota(jnp.int32, (SGU_CHUNK, SGU_CHUNK), 1)
    causal = r_i >= c_i

    def sgu_group(g):
        cols = slice(g * dg, (g + 1) * dg)
        v = _gelu_tanh(zf_ref[:, zv + g * dg:zv + (g + 1) * dg].astype(F32))
        mu = jnp.mean(v, axis=-1, keepdims=True)
        vc = v - mu
        vn = vc * lax.rsqrt(jnp.mean(vc * vc, axis=-1, keepdims=True) + EPS) * lng_ref[:, cols]
        vn = (vn + lnb_ref[:, cols]).astype(BF16)
        w = jnp.where(causal, ws_ref[g], 0.0).astype(BF16)
        for c in range(tm // SGU_CHUNK):
            rows = slice(c * SGU_CHUNK, (c + 1) * SGU_CHUNK)
            mixed = _dot(w, vn[rows, :]) + bs_ref[g]
            u = _gelu_tanh(zf_ref[rows, zu + g * dg:zu + (g + 1) * dg].astype(F32))
            ys_ref[rows, cols] = (u * mixed).astype(BF16)

    def xattn_head(h):
        cols = slice(h * dh, (h + 1) * dh)
        s = _dot(zf_ref[:, zq + h * dh:zq + (h + 1) * dh], kt_ref[cols, :])
        p = jnp.exp(s - jnp.max(s, axis=-1, keepdims=True))
        p = p / jnp.sum(p, axis=-1, keepdims=True)
        yx_ref[:, cols] = _dot(p.astype(BF16), v_ref[:, cols]).astype(BF16)

    half = d // 2

    def branch_half(j, y_ref, hf):
        cols = slice(hf * half, (hf + 1) * half)
        gate = _sigmoid(zg_ref[:, j * d + hf * half:j * d + (hf + 1) * half].astype(F32))
        term = gate * _dot(y_ref[...], wb_ref[j, :, cols])
        if j == 0:
            mg_ref[:, cols] = term
        else:
            mg_ref[:, cols] += term

    def out_half(hf):
        cols = slice(hf * half, (hf + 1) * half)
        yo_ref[:, cols] = _dot(mg_ref[...].astype(BF16), wo_ref[:, cols])

    sgu_group(0)
    branch_half(0, ym_ref, 0)
    sgu_group(1)
    branch_half(0, ym_ref, 1)
    sgu_group(2)
    branch_half(1, ysp_ref, 0)
    sgu_group(3)
    branch_half(1, ysp_ref, 1)
    xattn_head(0)
    branch_half(2, yxp_ref, 0)
    xattn_head(1)
    branch_half(2, yxp_ref, 1)
    xattn_head(2)
    out_half(0)
    xattn_head(3)
    out_half(1)
    o_ref[...] = h_ref[...] + _rms(yo_ref[...], g_ref[...])


def _mixb(h, z, y_m, kt, v, s, ln_g, ln_b, w_s, b_s, w_branch, w_out, norms, l):
    n, d = h.shape
    m = v.shape[1]
    tm = min(MIX_TM, s)
    nt = s // tm
    last = n // tm - 1
    assert z.shape[0] == 2 and z.shape[2] == N_BRANCH * d

    def cur(k):
        return jnp.minimum(k, last)

    def prev(k):
        return jnp.maximum(k - 1, 0)

    tok_prev = pl.BlockSpec((tm, d), lambda k: (prev(k), 0))
    return pl.pallas_call(
        functools.partial(_mixb_body, tm=tm, d=d),
        grid=(n // tm + 1,),
        in_specs=[
            pl.BlockSpec((None, tm, N_BRANCH * d), lambda k: (0, cur(k), 0)),
            pl.BlockSpec((None, d, m), lambda k: (cur(k) // nt, 0, 0)),
            pl.BlockSpec((None, m, d), lambda k: (cur(k) // nt, 0, 0)),
            pl.BlockSpec((None, tm, N_BRANCH * d), lambda k: (1, prev(k), 0)),
            tok_prev,
            tok_prev,
            _resident((None, 1, d), lambda k: (l, 0, 0)),
            _resident((None, 1, d), lambda k: (l, 0, 0)),
            _resident((None, SGU_GROUPS, SGU_CHUNK, SGU_CHUNK), lambda k: (l, 0, 0, 0)),
            _resident((None, SGU_GROUPS, SGU_CHUNK, 1), lambda k: (l, 0, 0, 0)),
            _resident((None, N_BRANCH, d, d), lambda k: (l, 0, 0, 0)),
            _resident((None, d, d), lambda k: (l, 0, 0)),
            _resident((None, 1, d), lambda k: (l * N_NORMS + 3, 0, 0)),
        ],
        out_specs=tok_prev,
        out_shape=jax.ShapeDtypeStruct((n, d), F32),
        scratch_shapes=[pltpu.VMEM((tm, d), BF16) for _ in range(4)] + [pltpu.VMEM((tm, d), F32) for _ in range(2)],
        compiler_params=_params(("arbitrary",)),
        name="mixb",
    )(z, kt, v, z, y_m, h, ln_g, ln_b, w_s, b_s, w_branch, w_out, norms)


def kernel(x, mem, norm_g, w_ffn_gu, w_ffn_down, w_in, conv_w, conv_b, b_i, b_f, mlstm_norm_g,
           sgu_ln_g, sgu_ln_b, sgu_w_s, sgu_b_s, w_mkv, w_branch, w_out):
    b, s, d = x.shape
    depth = norm_g.shape[0]
    dff = w_ffn_down.shape[2]
    n = b * s
    assert s % MIX_TM == 0 or s < MIX_TM
    assert dff % FFN_CW == 0

    norms = norm_g.reshape(depth * N_NORMS, 1, d)
    wgu = w_ffn_gu.astype(BF16)
    wd = w_ffn_down.astype(BF16)
    n_gate = 2 * NH_M
    c_if = N_Z_MLSTM * d
    w_in_bf = w_in.astype(BF16)
    w_mlstm = w_in_bf[:, :, :c_if]
    w_rest = w_in_bf[:, :, c_if + n_gate:]
    w_if = w_in_bf[:, :, c_if:c_if + GATE_LANES]
    b_if = jnp.pad(jnp.concatenate([b_i, b_f], axis=-1), ((0, 0), (0, GATE_LANES - n_gate)))
    b_if = b_if.reshape(depth, 1, GATE_LANES)
    wkv = w_mkv.astype(BF16)
    wb = w_branch.astype(BF16)
    wo = w_out.astype(BF16)
    conv_b3 = conv_b.reshape(depth, 1, d)
    ng3 = mlstm_norm_g.reshape(depth, 1, d)
    lng3 = sgu_ln_g.reshape(depth, 1, d)
    lnb3 = sgu_ln_b.reshape(depth, 1, d)
    bs4 = sgu_b_s.reshape(depth, SGU_GROUPS, SGU_CHUNK, 1)

    h = x.reshape(n, d)
    for l in range(depth):
        h = _ffn(h, norms, l, 0, 1, wgu, wd, 0)
        kt, v = _memkv(mem, norms, l, wkv)
        y_m, z = _mixa(h, b, s, norms, l, w_mlstm, w_rest, w_if, conv_w, conv_b3, b_if, ng3)
        h = _mixb(h, z, y_m, kt, v, s, lng3, lnb3, sgu_w_s, bs4, wb, wo, norms, l)
        h = _ffn(h, norms, l, 4, 5, wgu, wd, 1)
    return h.reshape(b, s, d)
```
